```python
import jax, jax.numpy as jnp
from jax import lax
import numpy as np

D_MODEL = 1024
BATCH = 4
SEQ = 8192
DEPTH = 1

GDN_HEADS = 4
GDN_HEAD_DIM = 128
GDN_WIDTH = GDN_HEADS * GDN_HEAD_DIM
GDN_CHUNK = 64
LRU_WIDTH = D_MODEL // 2
LRU_BLOCKS = 8
LRU_BLOCK = LRU_WIDTH // LRU_BLOCKS
LRU_C = 8.0
CONV_WIDTH = 4
N_EXPERTS = 32
TOP_K = 4
D_FF = D_MODEL
SWIGLU_ALPHA = 1.702
SWIGLU_LIMIT = 7.0
MOE_BLOCK = 256
EPS = 1e-6

IN_SPLITS = (GDN_WIDTH, GDN_WIDTH, GDN_WIDTH, GDN_WIDTH, GDN_HEADS, GDN_HEADS,
             LRU_WIDTH, LRU_WIDTH, D_MODEL, D_MODEL)
D_IN = sum(IN_SPLITS)

kernel_name = 'hybrid_gdn_rglru_moe_block'


def rms_norm(x, gain):
    xf = x.astype(jnp.float32)
    y = xf * lax.rsqrt(jnp.mean(xf * xf, axis=-1, keepdims=True) + EPS)
    return (y * gain.astype(jnp.float32)).astype(x.dtype)


def causal_depthwise_conv(x, w):
    return lax.conv_general_dilated(
        x, w[:, None, :].astype(x.dtype), window_strides=(1,),
        padding=[(CONV_WIDTH - 1, 0)], dimension_numbers=('NWC', 'WIO', 'NWC'),
        feature_group_count=x.shape[-1])


def l2_normalize(t):
    return t * lax.rsqrt(jnp.sum(t * t, axis=-1, keepdims=True) + EPS)


def _to_chunks(t):
    b, s, h = t.shape[:3]
    t = t.reshape(b, s // GDN_CHUNK, GDN_CHUNK, h, *t.shape[3:])
    return jnp.moveaxis(t, 3, 1)


def gated_delta_rule(q, k, v, g, beta):
    b, s, h, dk = q.shape
    dv = v.shape[-1]
    q = _to_chunks(q * (dk ** -0.5))
    k = _to_chunks(k)
    v = _to_chunks(v)
    g = _to_chunks(g)
    beta = _to_chunks(beta)
    c = GDN_CHUNK
    incl = jnp.tril(jnp.ones((c, c), bool))
    strict = jnp.tril(jnp.ones((c, c), bool), -1)
    gc = jnp.cumsum(g, axis=-1)
    diff = gc[..., :, None] - gc[..., None, :]
    decay = jnp.where(incl, jnp.exp(jnp.where(incl, diff, 0.0)), 0.0)
    kb = k * beta[..., None]
    a_mat = jnp.where(strict, jnp.einsum('bhncd,bhnsd->bhncs', kb, k) * decay, 0.0)
    eye = jnp.eye(c, dtype=q.dtype)
    t_mat = lax.linalg.triangular_solve(eye + a_mat, jnp.broadcast_to(eye, a_mat.shape),
                                        left_side=True, lower=True, unit_diagonal=True)
    u = jnp.matmul(t_mat, v * beta[..., None])
    w = jnp.matmul(t_mat, kb * jnp.exp(gc)[..., None])
    qk = jnp.where(incl, jnp.einsum('bhncd,bhnsd->bhncs', q, k) * decay, 0.0)
    q_dec = q * jnp.exp(gc)[..., None]
    k_dec = k * jnp.exp(gc[..., -1:] - gc)[..., None]
    g_last = jnp.exp(gc[..., -1])
    xs = tuple(jnp.moveaxis(t, 2, 0) for t in (w, u, q_dec, k_dec, qk, g_last))

    def step(state, inp):
        w_c, u_c, qd_c, kd_c, qk_c, gl_c = inp
        v_new = u_c - jnp.einsum('bhcd,bhde->bhce', w_c, state)
        o_c = (jnp.einsum('bhcd,bhde->bhce', qd_c, state)
               + jnp.einsum('bhcs,bhse->bhce', qk_c, v_new))
        state = state * gl_c[..., None, None] + jnp.einsum('bhcd,bhce->bhde', kd_c, v_new)
        return state, o_c

    state0 = jnp.zeros((b, h, dk, dv), q.dtype)
    _, o = lax.scan(step, state0, xs)
    return jnp.transpose(o, (1, 0, 3, 2, 4)).reshape(b, s, h, dv)


def gdn_branch(q, k, v, z, b_in, a_in, conv_w, a_log, dt_bias, norm_w):
    bsz, s, _ = q.shape
    qkv = jax.nn.silu(causal_depthwise_conv(jnp.concatenate([q, k, v], axis=-1), conv_w))
    q, k, v = jnp.split(qkv, 3, axis=-1)
    heads = lambda t: t.reshape(bsz, s, GDN_HEADS, GDN_HEAD_DIM).astype(jnp.float32)
    q = l2_normalize(heads(q))
    k = l2_normalize(heads(k))
    v = heads(v)
    beta = jax.nn.sigmoid(b_in.astype(jnp.float32))
    g = -jnp.exp(a_log.astype(jnp.float32)) * jax.nn.softplus(
        a_in.astype(jnp.float32) + dt_bias.astype(jnp.float32))
    o = gated_delta_rule(q, k, v, g, beta)
    o = (o * lax.rsqrt(jnp.mean(o * o, axis=-1, keepdims=True) + EPS)
         * norm_w.astype(jnp.float32) * jax.nn.silu(heads(z)))
    return o.reshape(bsz, s, GDN_WIDTH).astype(q_dtype_of(z))


def q_dtype_of(t):
    return t.dtype


def rglru_branch(xb, yb, conv_w, conv_b, w_a, b_a, w_x, b_x, lam):
    bsz, s, _ = xb.shape
    xc = causal_depthwise_conv(xb, conv_w) + conv_b
    xblk = xc.reshape(bsz, s, LRU_BLOCKS, LRU_BLOCK)
    r = jax.nn.sigmoid((jnp.einsum('bsnc,ncd->bsnd', xblk, w_a).reshape(bsz, s, LRU_WIDTH)
                        + b_a).astype(jnp.float32))
    i = jax.nn.sigmoid((jnp.einsum('bsnc,ncd->bsnd', xblk, w_x).reshape(bsz, s, LRU_WIDTH)
                        + b_x).astype(jnp.float32))
    log_a = -LRU_C * r * jax.nn.softplus(-lam.astype(jnp.float32))
    a = jnp.exp(log_a)
    mult = jnp.sqrt(jnp.maximum(-jnp.expm1(2.0 * log_a), 0.0))
    bterm = mult * (i * xc.astype(jnp.float32))

    def combine(left, right):
        a_l, b_l = left
        a_r, b_r = right
        return a_l * a_r, a_r * b_l + b_r

    _, hseq = lax.associative_scan(combine, (a, bterm), axis=1)
    y = hseq * jax.nn.gelu(yb.astype(jnp.float32))
    return y.astype(xb.dtype)


def swiglu_clamped(u):
    u_glu = jnp.minimum(u[..., ::2], SWIGLU_LIMIT)
    u_lin = jnp.clip(u[..., 1::2], -SWIGLU_LIMIT, SWIGLU_LIMIT)
    return u_glu * jax.nn.sigmoid(SWIGLU_ALPHA * u_glu) * (u_lin + 1.0)


def moe_ffn(h, w_router, b_router, w1, b1, w2, b2):
    bsz, s, d = h.shape
    n_tok = bsz * s
    hf = h.reshape(n_tok, d)
    logits = hf.astype(jnp.float32) @ w_router.astype(jnp.float32) + b_router.astype(jnp.float32)
    top_val, top_idx = lax.top_k(logits, TOP_K)
    gate = jax.nn.softmax(top_val, axis=-1)
    n_assign = n_tok * TOP_K
    flat_e = top_idx.reshape(-1).astype(jnp.int32)
    flat_tok = jnp.arange(n_assign, dtype=jnp.int32) // TOP_K
    flat_w = gate.reshape(-1)
    order = jnp.argsort(flat_e)
    sorted_e = flat_e[order]
    counts = jnp.bincount(flat_e, length=N_EXPERTS)
    padded = (counts + MOE_BLOCK - 1) // MOE_BLOCK * MOE_BLOCK
    start = jnp.cumsum(counts) - counts
    pstart = jnp.cumsum(padded) - padded
    dest = pstart[sorted_e] + jnp.arange(n_assign, dtype=jnp.int32) - start[sorted_e]
    n_blocks = -(-n_assign // MOE_BLOCK) + N_EXPERTS
    n_slots = n_blocks * MOE_BLOCK
    slot_tok = jnp.full((n_slots,), n_tok, jnp.int32).at[dest].set(flat_tok[order])
    slot_w = jnp.zeros((n_slots,), h.dtype).at[dest].set(flat_w[order].astype(h.dtype))
    block_e = jnp.minimum(
        jnp.searchsorted(jnp.cumsum(padded), jnp.arange(n_blocks) * MOE_BLOCK, side='right'),
        N_EXPERTS - 1)
    h_pad = jnp.concatenate([hf, jnp.zeros((1, d), h.dtype)], axis=0)
    xs = h_pad[slot_tok].reshape(n_blocks, MOE_BLOCK, d)

    def expert_block(args):
        xblk, e = args
        u = xblk @ w1[e] + b1[e]
        return swiglu_clamped(u) @ w2[e] + b2[e]

    ys = lax.map(expert_block, (xs, block_e)).reshape(n_slots, d)
    out = jnp.zeros((n_tok + 1, d), h.dtype).at[slot_tok].add(ys * slot_w[:, None])
    return out[:n_tok].reshape(bsz, s, d)


def setup_inputs(seed: int = 0) -> dict:
    key = jax.random.key(seed)
    ks = jax.random.split(key, 32)
    f32 = jnp.float32
    nrm = lambda k, shape, scale: jax.random.normal(k, shape, f32) * scale
    L, D, E, F = DEPTH, D_MODEL, N_EXPERTS, D_FF
    a_init = jax.random.uniform(ks[8], (L, GDN_HEADS), f32, 1.0, 16.0)
    dt = jnp.exp(jax.random.uniform(ks[9], (L, GDN_HEADS), f32, float(np.log(1e-3)), float(np.log(0.1))))
    a_pow = jax.random.uniform(ks[17], (L, LRU_WIDTH), f32, 0.9, 0.999)
    a_lru = a_pow ** (1.0 / LRU_C)
    return {
        'x': nrm(ks[0], (BATCH, SEQ, D), 1.0),
        'c': nrm(ks[1], (BATCH, D), 1.0),
        'w_ada': nrm(ks[2], (L, D, 6 * D), D ** -0.5),
        'b_ada': nrm(ks[3], (L, 6 * D), 0.01),
        'norm_mix': 1.0 + nrm(ks[4], (L, D), 0.01),
        'norm_ffn': 1.0 + nrm(ks[5], (L, D), 0.01),
        'w_in': nrm(ks[6], (L, D, D_IN), D ** -0.5),
        'gdn_conv_w': nrm(ks[7], (L, CONV_WIDTH, 3 * GDN_WIDTH), CONV_WIDTH ** -0.5),
        'gdn_a_log': jnp.log(a_init),
        'gdn_dt_bias': dt + jnp.log(-jnp.expm1(-dt)),
        'gdn_norm_w': 1.0 + nrm(ks[10], (L, GDN_HEAD_DIM), 0.01),
        'lru_conv_w': nrm(ks[11], (L, CONV_WIDTH, LRU_WIDTH), CONV_WIDTH ** -0.5),
        'lru_conv_b': nrm(ks[12], (L, LRU_WIDTH), 0.01),
        'lru_w_a': nrm(ks[13], (L, LRU_BLOCKS, LRU_BLOCK, LRU_BLOCK), LRU_BLOCK ** -0.5),
        'lru_b_a': nrm(ks[14], (L, LRU_WIDTH), 0.01),
        'lru_w_x': nrm(ks[15], (L, LRU_BLOCKS, LRU_BLOCK, LRU_BLOCK), LRU_BLOCK ** -0.5),
        'lru_b_x': nrm(ks[16], (L, LRU_WIDTH), 0.01),
        'lru_lambda': jnp.log(a_lru) - jnp.log1p(-a_lru),
        'w_branch_gdn': nrm(ks[18], (L, GDN_WIDTH, D), GDN_WIDTH ** -0.5),
        'w_branch_lru': nrm(ks[19], (L, LRU_WIDTH, D), LRU_WIDTH ** -0.5),
        'w_out': nrm(ks[20], (L, D, D), D ** -0.5),
        'w_router': nrm(ks[21], (L, D, E), D ** -0.5),
        'b_router': nrm(ks[22], (L, E), 0.01),
        'w1': nrm(ks[23], (L, E, D, 2 * F), D ** -0.5),
        'b1': nrm(ks[24], (L, E, 2 * F), 0.01),
        'w2': nrm(ks[25], (L, E, F, D), F ** -0.5),
        'b2': nrm(ks[26], (L, E, D), 0.01),
        'norm_final': 1.0 + nrm(ks[27], (D,), 0.01),
    }


def reference(x, c, w_ada, b_ada, norm_mix, norm_ffn, w_in, gdn_conv_w, gdn_a_log, gdn_dt_bias,
              gdn_norm_w, lru_conv_w, lru_conv_b, lru_w_a, lru_b_a, lru_w_x, lru_b_x, lru_lambda,
              w_branch_gdn, w_branch_lru, w_out, w_router, b_router, w1, b1, w2, b2, norm_final):
    split_pts = tuple(int(p) for p in np.cumsum(IN_SPLITS)[:-1])
    c_act = jax.nn.silu(c)
    for l in range(DEPTH):
        mod = c_act @ w_ada[l] + b_ada[l]
        shift_m, scale_m, gate_m, shift_f, scale_f, gate_f = [
            m[:, None, :] for m in jnp.split(mod, 6, axis=-1)]
        h = rms_norm(x, norm_mix[l]) * (1.0 + scale_m) + shift_m
        proj = h @ w_in[l]
        q, k, v, z, b_in, a_in, xb, yb, g_gdn, g_lru = jnp.split(proj, split_pts, axis=-1)
        o_gdn = gdn_branch(q, k, v, z, b_in, a_in, gdn_conv_w[l], gdn_a_log[l], gdn_dt_bias[l],
                           gdn_norm_w[l])
        o_lru = rglru_branch(xb, yb, lru_conv_w[l], lru_conv_b[l], lru_w_a[l], lru_b_a[l],
                             lru_w_x[l], lru_b_x[l], lru_lambda[l])
        merged = (jax.nn.sigmoid(g_gdn) * (o_gdn @ w_branch_gdn[l])
                  + jax.nn.sigmoid(g_lru) * (o_lru @ w_branch_lru[l]))
        x = x + gate_m * (merged @ w_out[l])
        h = rms_norm(x, norm_ffn[l]) * (1.0 + scale_f) + shift_f
        x = x + gate_f * moe_ffn(h, w_router[l], b_router[l], w1[l], b1[l], w2[l], b2[l])
    return rms_norm(x, norm_final)
```

```python
import functools

import jax
import jax.numpy as jnp
from jax import lax
from jax.experimental import pallas as pl
from jax.experimental.pallas import tpu as pltpu

F32 = jnp.float32
BF16 = jnp.bfloat16

EPS = 1e-6
GDN_HEADS = 4
GDN_HEAD_DIM = 128
GDN_WIDTH = GDN_HEADS * GDN_HEAD_DIM
GDN_CHUNK = 64
LRU_WIDTH = 512
LRU_C = 8.0
CONV_WIDTH = 4
N_EXPERTS = 32
TOP_K = 4
SWIGLU_ALPHA = 1.702
SWIGLU_LIMIT = 7.0
MOE_BLOCK = 256

HALO = 8
VMEM_LIMIT = 56 * 1024 * 1024


def _cparams(sem):
    return pltpu.CompilerParams(dimension_semantics=sem, vmem_limit_bytes=VMEM_LIMIT)


def _sigmoid(x):
    return 1.0 / (1.0 + jnp.exp(-x))


def _silu(x):
    return x * _sigmoid(x)


def _softplus(x):
    return jnp.maximum(x, 0.0) + jnp.log1p(jnp.exp(-jnp.abs(x)))


def _gelu_tanh(x):
    return 0.5 * x * (1.0 + jnp.tanh(0.7978845608028654 * (x + 0.044715 * (x * x * x))))


def _bdot(a, b):
    return jnp.dot(a.astype(BF16), b.astype(BF16), preferred_element_type=F32)


def _bdot_nt(a, b):
    return lax.dot_general(a.astype(BF16), b.astype(BF16), (((1,), (1,)), ((), ())),
                           preferred_element_type=F32)


def _bdot_tn(a, b):
    return lax.dot_general(a.astype(BF16), b.astype(BF16), (((0,), (0,)), ((), ())),
                           preferred_element_type=F32)


def _fdot(a, b):
    return jnp.dot(a, b, preferred_element_type=F32, precision=lax.Precision.HIGHEST)


def _ada_kernel(ct_ref, w_ref, b_ref, o_ref):
    ct = _silu(ct_ref[...])
    w = w_ref[...]
    for b in range(ct.shape[1]):
        o_ref[b:b + 1, :] = jnp.sum(ct[:, b:b + 1] * w, axis=0, keepdims=True) + b_ref[...]


def _ada(c, w_ada, b_ada):
    bsz, d = c.shape
    n = w_ada.shape[1]
    tn = 512
    return pl.pallas_call(
        _ada_kernel,
        grid=(n // tn,),
        in_specs=[pl.BlockSpec((d, bsz), lambda j: (0, 0)),
                  pl.BlockSpec((d, tn), lambda j: (0, j)),
                  pl.BlockSpec((1, tn), lambda j: (0, j))],
        out_specs=pl.BlockSpec((bsz, tn), lambda j: (0, j)),
        out_shape=jax.ShapeDtypeStruct((bsz, n), F32),
        compiler_params=_cparams(("arbitrary",)),
        name="ada",
    )(c.T, w_ada, b_ada.reshape(1, n))


def _inproj_kernel(x_ref, mod_ref, nw_ref, wm_ref, wba_ref,
                   qkv_ref, z_ref, xb_ref, yb_ref, gg_ref, gl_ref, ba_ref, bat_ref):
    x = x_ref[0]
    shift = mod_ref[0, 0:1, :]
    scale = mod_ref[0, 1:2, :]
    y = x * lax.rsqrt(jnp.mean(x * x, axis=-1, keepdims=True) + EPS)
    h = ((y * nw_ref[...]) * (1.0 + scale) + shift).astype(BF16)
    col = 0
    for ref in (qkv_ref, z_ref, xb_ref, yb_ref, gg_ref, gl_ref):
        n = ref.shape[-1]
        ref[0] = jnp.dot(h, wm_ref[:, col:col + n], preferred_element_type=F32).astype(ref.dtype)
        col += n
    wba = wba_ref[...]
    nt = (((1,), (1,)), ((), ()))
    bat_ref[0] = lax.dot_general(wba[0:8], h, nt, preferred_element_type=F32)
    ba_ref[0] = lax.dot_general(h, wba, nt, preferred_element_type=F32)[:, 0:8]


def _inproj(x, mod, norm_w, w_main, w_ba_t, ts=256):
    bsz, s, d = x.shape
    widths = (3 * GDN_WIDTH, GDN_WIDTH, LRU_WIDTH, LRU_WIDTH, d, d)
    tok = lambda n: pl.BlockSpec((1, ts, n), lambda b, i: (b, i, 0))
    out_shapes = [jax.ShapeDtypeStruct((bsz, s, n), BF16) for n in widths]
    out_shapes += [jax.ShapeDtypeStruct((bsz, s, 8), F32), jax.ShapeDtypeStruct((bsz, 8, s), F32)]
    out_specs = [tok(n) for n in widths]
    out_specs += [tok(8), pl.BlockSpec((1, 8, ts), lambda b, i: (b, 0, i))]
    return pl.pallas_call(
        _inproj_kernel,
        grid=(bsz, s // ts),
        in_specs=[tok(d),
                  pl.BlockSpec((1, 6, d), lambda b, i: (b, 0, 0)),
                  pl.BlockSpec((1, d), lambda b, i: (0, 0)),
                  pl.BlockSpec(w_main.shape, lambda b, i: (0, 0)),
                  pl.BlockSpec(w_ba_t.shape, lambda b, i: (0, 0))],
        out_specs=out_specs,
        out_shape=out_shapes,
        compiler_params=_cparams(("arbitrary", "arbitrary")),
        name="inproj",
    )(x, mod, norm_w, w_main, w_ba_t)


def _causal_conv(x_tile, w_ref, xbuf, first):
    ts = x_tile.shape[0]

    @pl.when(first)
    def _():
        xbuf[0:HALO, :] = jnp.zeros((HALO, xbuf.shape[1]), F32)

    xbuf[HALO:HALO + ts, :] = x_tile
    acc = None
    for j in range(CONV_WIDTH):
        off = HALO - (CONV_WIDTH - 1) + j
        term = w_ref[j:j + 1, :] * xbuf[off:off + ts, :]
        acc = term if acc is None else acc + term
    xbuf[0:HALO, :] = xbuf[ts:ts + HALO, :]
    return acc


def _gdn_kernel(qkv_ref, z_ref, ba_ref, bat_ref, cw_ref, hrow_ref, hcol_ref, nw_ref, lvl_ref,
                o_ref, xbuf, s_ref):
    ts = qkv_ref.shape[1]
    c = GDN_CHUNK
    dk = GDN_HEAD_DIM
    first = pl.program_id(1) == 0

    @pl.when(first)
    def _():
        s_ref[...] = jnp.zeros(s_ref.shape, F32)

    qkv = _silu(_causal_conv(qkv_ref[0].astype(F32), cw_ref, xbuf, first))

    ba = ba_ref[0]
    g_col = -jnp.exp(hrow_ref[0:1, :]) * _softplus(ba + hrow_ref[1:2, :])
    beta_col = _sigmoid(ba)
    r_i = lax.broadcasted_iota(jnp.int32, (ts, ts), 0)
    c_i = lax.broadcasted_iota(jnp.int32, (ts, ts), 1)
    same = (r_i // c) == (c_i // c)
    incl = same & (c_i <= r_i)
    strict = same & (c_i < r_i)
    gc_col = _fdot(jnp.where(incl, 1.0, 0.0).astype(F32), g_col)
    g_row = -jnp.exp(hcol_ref[:, 0:1]) * _softplus(bat_ref[0] + hcol_ref[:, 1:2])
    gc_row = _fdot(g_row, jnp.where(same & (r_i <= c_i), 1.0, 0.0).astype(F32))
    eye = jnp.where(r_i == c_i, 1.0, 0.0).astype(F32)
    scale = dk ** -0.5
    nw = nw_ref[...]
    n_lvl = lvl_ref.shape[0]

    for h in range(GDN_HEADS):
        q = qkv[:, h * dk:(h + 1) * dk]
        k = qkv[:, GDN_WIDTH + h * dk:GDN_WIDTH + (h + 1) * dk]
        v = qkv[:, 2 * GDN_WIDTH + h * dk:2 * GDN_WIDTH + (h + 1) * dk]
        q = q * (lax.rsqrt(jnp.sum(q * q, axis=-1, keepdims=True) + EPS) * scale)
        k = k * lax.rsqrt(jnp.sum(k * k, axis=-1, keepdims=True) + EPS)
        zg = _silu(z_ref[0, :, h * dk:(h + 1) * dk].astype(F32))
        beta = beta_col[:, h:h + 1]
        gcc = gc_col[:, GDN_HEADS + h:GDN_HEADS + h + 1]
        gcr = gc_row[GDN_HEADS + h:GDN_HEADS + h + 1, :]
        decay = jnp.exp(jnp.where(incl, gcc - gcr, 0.0))
        kb = k * beta
        m1 = _bdot_nt(jnp.concatenate([kb, q], axis=0), k)
        a_mat = jnp.where(strict, m1[:ts] * decay, 0.0)
        qk = jnp.where(incl, m1[ts:] * decay, 0.0)
        a_bf = a_mat.astype(BF16)
        t_mat = eye - a_mat * lvl_ref[0].astype(F32)
        for lv in range(1, n_lvl):
            t_bf = t_mat.astype(BF16)
            t_mat = t_mat - _bdot(t_bf, _bdot(a_bf * lvl_ref[lv], t_bf))
        e_gc = jnp.exp(gcc)
        uw = _bdot(t_mat, jnp.concatenate([v * beta, kb * e_gc], axis=1))
        q_dec = q * e_gc
        state = s_ref[h]
        for n in range(ts // c):
            rows = slice(n * c, (n + 1) * c)
            gc_last = gcc[(n + 1) * c - 1:(n + 1) * c, :]
            k_dec = k[rows] * jnp.exp(gc_last - gcc[rows])
            ws = _bdot(jnp.concatenate([uw[rows, dk:], q_dec[rows]], axis=0), state)
            v_new = uw[rows, :dk] - ws[:c]
            o = ws[c:] + _bdot(qk[rows, n * c:(n + 1) * c], v_new)
            state = state * jnp.exp(gc_last) + _bdot_tn(k_dec, v_new)
            o = o * lax.rsqrt(jnp.mean(o * o, axis=-1, keepdims=True) + EPS) * nw * zg[rows]
            o_ref[0, rows, h * dk:(h + 1) * dk] = o.astype(o_ref.dtype)
        s_ref[h] = state


def _gdn_level_masks(ts):
    i = jnp.arange(ts)[:, None]
    j = jnp.arange(ts)[None, :]
    masks = []
    s = 1
    while s < GDN_CHUNK:
        masks.append(((i // (2 * s)) == (j // (2 * s))) & ((i // s) != (j // s)))
        s *= 2
    return jnp.stack(masks).astype(BF16)


def _gdn(qkv, z, ba, bat, conv_w, hrow, hcol, norm_w, ts=256):
    bsz, s, _ = qkv.shape
    lvl = _gdn_level_masks(ts)
    tok = lambda n: pl.BlockSpec((1, ts, n), lambda b, i: (b, i, 0))
    full = lambda a: pl.BlockSpec(a.shape, lambda b, i: (0,) * a.ndim)
    return pl.pallas_call(
        _gdn_kernel,
        grid=(bsz, s // ts),
        in_specs=[tok(3 * GDN_WIDTH), tok(GDN_WIDTH), tok(8),
                  pl.BlockSpec((1, 8, ts), lambda b, i: (b, 0, i)),
                  full(conv_w), full(hrow), full(hcol), full(norm_w), full(lvl)],
        out_specs=tok(GDN_WIDTH),
        out_shape=jax.ShapeDtypeStruct((bsz, s, GDN_WIDTH), BF16),
        scratch_shapes=[pltpu.VMEM((ts + HALO, 3 * GDN_WIDTH), F32),
                        pltpu.VMEM((GDN_HEADS, GDN_HEAD_DIM, GDN_HEAD_DIM), F32)],
        compiler_params=_cparams(("arbitrary", "arbitrary")),
        name="gdn",
    )(qkv, z, ba, bat, conv_w, hrow, hcol, norm_w, lvl)


def _lru_kernel(xb_ref, yb_ref, cw_ref, cb_ref, wa_ref, ba_ref, wx_ref, bx_ref, lam_ref,
                o_ref, xbuf, carry_ref):
    ts = xb_ref.shape[1]
    first = pl.program_id(1) == 0

    @pl.when(first)
    def _():
        carry_ref[...] = jnp.zeros(carry_ref.shape, F32)

    xc = _causal_conv(xb_ref[0].astype(F32), cw_ref, xbuf, first) + cb_ref[...]
    r = _sigmoid(_bdot(xc, wa_ref[...]) + ba_ref[...])
    i = _sigmoid(_bdot(xc, wx_ref[...]) + bx_ref[...])
    log_a = (-LRU_C) * r * _softplus(-lam_ref[...])
    a = jnp.exp(log_a)
    mult = jnp.sqrt(jnp.maximum(1.0 - jnp.exp(2.0 * log_a), 0.0))
    b = mult * (i * xc)
    row = lax.broadcasted_iota(jnp.int32, a.shape, 0)
    d = 1
    while d < ts:
        keep = row >= d
        a_s = jnp.where(keep, pltpu.roll(a, d, axis=0), 1.0)
        b_s = jnp.where(keep, pltpu.roll(b, d, axis=0), 0.0)
        b = a * b_s + b
        a = a * a_s
        d *= 2
    h = a * carry_ref[...] + b
    carry_ref[...] = h[ts - 1:ts, :]
    o_ref[0] = (h * _gelu_tanh(yb_ref[0].astype(F32))).astype(o_ref.dtype)


def _lru(xb, yb, conv_w, conv_b, wa, b_a, wx, b_x, lam, ts=256):
    bsz, s, w = xb.shape
    tok = pl.BlockSpec((1, ts, w), lambda b, i: (b, i, 0))
    full = lambda a: pl.BlockSpec(a.shape, lambda b, i: (0,) * a.ndim)
    args = (conv_w, conv_b, wa, b_a, wx, b_x, lam)
    return pl.pallas_call(
        _lru_kernel,
        grid=(bsz, s // ts),
        in_specs=[tok, tok] + [full(a) for a in args],
        out_specs=tok,
        out_shape=jax.ShapeDtypeStruct((bsz, s, w), BF16),
        scratch_shapes=[pltpu.VMEM((ts + HALO, w), F32), pltpu.VMEM((1, w), F32)],
        compiler_params=_cparams(("arbitrary", "arbitrary")),
        name="lru",
    )(xb, yb, *args)


def _merge_kernel(x_ref, og_ref, ol_ref, gg_ref, gl_ref, mod_ref, nw_ref,
                  wg_ref, wl_ref, wo_ref, wrt_ref, br_ref,
                  x1_ref, h2_ref, idx_ref, gate_ref, rank_ref, cnt_ref, carry_ref):
    tm = x_ref.shape[1]
    step = pl.program_id(0) * pl.num_programs(1) + pl.program_id(1)

    @pl.when(step == 0)
    def _():
        carry_ref[...] = jnp.zeros(carry_ref.shape, F32)

    gate_m = mod_ref[0, 2:3, :]
    shift_f = mod_ref[0, 3:4, :]
    scale_f = mod_ref[0, 4:5, :]
    merged = (_sigmoid(gg_ref[0].astype(F32)) * jnp.dot(og_ref[0], wg_ref[...], preferred_element_type=F32)
              + _sigmoid(gl_ref[0].astype(F32)) * jnp.dot(ol_ref[0], wl_ref[...], preferred_element_type=F32))
    x1 = x_ref[0] + gate_m * _bdot(merged, wo_ref[...])
    x1_ref[0] = x1
    y = x1 * lax.rsqrt(jnp.mean(x1 * x1, axis=-1, keepdims=True) + EPS)
    h2 = (y * nw_ref[...]) * (1.0 + scale_f) + shift_f
    h2_ref[...] = h2.reshape(tm, 1, h2.shape[-1])

    logits = lax.dot_general(wrt_ref[...], h2, (((1,), (1,)), ((), ())),
                             preferred_element_type=F32, precision=lax.Precision.HIGHEST) + br_ref[...]
    e_iota = lax.broadcasted_iota(jnp.int32, logits.shape, 0)
    work = logits
    sel = jnp.zeros(logits.shape, F32)
    vals, idxs = [], []
    for _ in range(TOP_K):
        m = jnp.max(work, axis=0, keepdims=True)
        idx = jnp.min(jnp.where(work == m, e_iota, N_EXPERTS), axis=0, keepdims=True)
        hit = e_iota == idx
        sel = jnp.where(hit, 1.0, sel)
        work = jnp.where(hit, -jnp.inf, work)
        vals.append(m)
        idxs.append(idx)
    exps = [jnp.exp(v - vals[0]) for v in vals]
    denom = exps[0] + exps[1] + exps[2] + exps[3]
    r_i = lax.broadcasted_iota(jnp.int32, (tm, tm), 0)
    c_i = lax.broadcasted_iota(jnp.int32, (tm, tm), 1)
    before = jnp.where(r_i < c_i, 1.0, 0.0).astype(BF16)
    prefix = jnp.dot(sel.astype(BF16), before, preferred_element_type=F32) + carry_ref[...]
    for kk in range(TOP_K):
        idx_ref[0, kk:kk + 1, :] = idxs[kk]
        gate_ref[0, kk:kk + 1, :] = exps[kk] / denom
        rank = jnp.sum(jnp.where(e_iota == idxs[kk], prefix, 0.0), axis=0, keepdims=True)
        rank_ref[0, kk:kk + 1, :] = rank.astype(jnp.int32)
    total = carry_ref[...] + jnp.sum(sel, axis=1, keepdims=True)
    carry_ref[...] = total
    cnt_ref[...] = total.astype(jnp.int32)


def _merge(x, og, ol, gg, gl, mod, norm_w, wg, wl, wo, wrt, br, tm=256):
    bsz, s, d = x.shape
    ns = s // tm
    tok = lambda n: pl.BlockSpec((1, tm, n), lambda b, i: (b, i, 0))
    full = lambda a: pl.BlockSpec(a.shape, lambda b, i: (0,) * a.ndim)
    lane = pl.BlockSpec((1, TOP_K, tm), lambda b, i: (b * ns + i, 0, 0))
    nt = bsz * ns
    return pl.pallas_call(
        _merge_kernel,
        grid=(bsz, ns),
        in_specs=[tok(d), tok(GDN_WIDTH), tok(LRU_WIDTH), tok(d), tok(d),
                  pl.BlockSpec((1, 6, d), lambda b, i: (b, 0, 0)),
                  full(norm_w), full(wg), full(wl), full(wo), full(wrt), full(br)],
        out_specs=[tok(d),
                   pl.BlockSpec((tm, 1, d), lambda b, i: (b * ns + i, 0, 0)),
                   lane, lane, lane,
                   pl.BlockSpec((N_EXPERTS, 1), lambda b, i: (0, 0))],
        out_shape=[jax.ShapeDtypeStruct((bsz, s, d), F32),
                   jax.ShapeDtypeStruct((bsz * s, 1, d), F32),
                   jax.ShapeDtypeStruct((nt, TOP_K, tm), jnp.int32),
                   jax.ShapeDtypeStruct((nt, TOP_K, tm), F32),
                   jax.ShapeDtypeStruct((nt, TOP_K, tm), jnp.int32),
                   jax.ShapeDtypeStruct((N_EXPERTS, 1), jnp.int32)],
        scratch_shapes=[pltpu.VMEM((N_EXPERTS, 1), F32)],
        compiler_params=_cparams(("arbitrary", "arbitrary")),
        name="merge",
    )(x, og, ol, gg, gl, mod, norm_w, wg, wl, wo, wrt, br)


def _dispatch_kernel(dest_ref, pad_ref, h_ref, xs_ref, zrow, sem, zsem):
    tm = h_ref.shape[0]
    n_pad = pad_ref.shape[-1]

    def row_copy(r, slot):
        return pltpu.make_async_copy(h_ref.at[r], xs_ref.at[slot], sem)

    def issue(r, carry):
        for kk in range(TOP_K):
            row_copy(r, dest_ref[0, 0, kk * tm + r]).start()
        return carry

    lax.fori_loop(0, tm, issue, 0)

    @pl.when(pl.program_id(0) == 0)
    def _():
        zrow[...] = jnp.zeros(zrow.shape, F32)

        def zissue(j, carry):
            pltpu.make_async_copy(zrow.at[0], xs_ref.at[pad_ref[0, 0, j]], zsem).start()
            return carry

        lax.fori_loop(0, n_pad, zissue, 0)

        def zwait(j, carry):
            pltpu.make_async_copy(zrow.at[0], xs_ref.at[0], zsem).wait()
            return carry

        lax.fori_loop(0, n_pad, zwait, 0)

    def wait(r, carry):
        for kk in range(TOP_K):
            row_copy(r, 0).wait()
        return carry

    lax.fori_loop(0, tm, wait, 0)


def _dispatch(h2_rows, dest, pad_slots, n_slots, tm=256):
    n_tok, _, d = h2_rows.shape
    nt = n_tok // tm
    return pl.pallas_call(
        _dispatch_kernel,
        grid=(nt,),
        in_specs=[pl.BlockSpec((1, 1, TOP_K * tm), lambda i: (i, 0, 0), memory_space=pltpu.SMEM),
                  pl.BlockSpec(pad_slots.shape, lambda i: (0, 0, 0), memory_space=pltpu.SMEM),
                  pl.BlockSpec((tm, 1, d), lambda i: (i, 0, 0))],
        out_specs=pl.BlockSpec(memory_space=pl.ANY),
        out_shape=jax.ShapeDtypeStruct((n_slots, 1, d), F32),
        scratch_shapes=[pltpu.VMEM((1, 1, d), F32), pltpu.SemaphoreType.DMA, pltpu.SemaphoreType.DMA],
        compiler_params=_cparams(("arbitrary",)),
        name="dispatch",
    )(dest, pad_slots, h2_rows)


def _expert_kernel(be_ref, nu_ref, xs_ref, w1g_ref, w1l_ref, b1g_ref, b1l_ref, w2_ref, b2_ref, ys_ref):
    blk, _, d = xs_ref.shape

    @pl.when(pl.program_id(0) < nu_ref[0])
    def _():
        x = xs_ref[...].reshape(blk, d).astype(BF16)
        u_glu = jnp.dot(x, w1g_ref[0], preferred_element_type=F32) + b1g_ref[0]
        u_lin = jnp.dot(x, w1l_ref[0], preferred_element_type=F32) + b1l_ref[0]
        u_glu = jnp.minimum(u_glu, SWIGLU_LIMIT)
        u_lin = jnp.clip(u_lin, -SWIGLU_LIMIT, SWIGLU_LIMIT)
        act = u_glu * _sigmoid(SWIGLU_ALPHA * u_glu) * (u_lin + 1.0)
        y = jnp.dot(act.astype(BF16), w2_ref[0], preferred_element_type=F32) + b2_ref[0]
        ys_ref[...] = y.reshape(blk, 1, d)

    @pl.when(pl.program_id(0) >= nu_ref[0])
    def _():
        ys_ref[...] = jnp.zeros(ys_ref.shape, F32)


def _experts(block_e, n_used, xs, w1g, w1l, b1g, b1l, w2, b2):
    n_slots, _, d = xs.shape
    f = w1g.shape[-1]
    n_blocks = n_slots // MOE_BLOCK
    slot = lambda b, be, nu: (jnp.minimum(b, nu[0] - 1), 0, 0)
    wsel = lambda b, be, nu: (be[b], 0, 0)
    grid_spec = pltpu.PrefetchScalarGridSpec(
        num_scalar_prefetch=2,
        grid=(n_blocks,),
        in_specs=[pl.BlockSpec((MOE_BLOCK, 1, d), slot),
                  pl.BlockSpec((1, d, f), wsel), pl.BlockSpec((1, d, f), wsel),
                  pl.BlockSpec((1, 1, f), wsel), pl.BlockSpec((1, 1, f), wsel),
                  pl.BlockSpec((1, f, d), wsel), pl.BlockSpec((1, 1, d), wsel)],
        out_specs=pl.BlockSpec((MOE_BLOCK, 1, d), lambda b, be, nu: (b, 0, 0)),
    )
    return pl.pallas_call(
        _expert_kernel,
        grid_spec=grid_spec,
        out_shape=jax.ShapeDtypeStruct((n_slots, 1, d), F32),
        compiler_params=_cparams(("arbitrary",)),
        name="experts",
    )(block_e, n_used, xs, w1g, w1l, b1g, b1l, w2, b2)


def _combine_kernel(dest_ref, x1_ref, gate_ref, mod_ref, nw_ref, ys_ref, o_ref, buf3, buf2, sem):
    tm = x1_ref.shape[1]
    d = x1_ref.shape[2]

    def row_copy(j, slot):
        return pltpu.make_async_copy(ys_ref.at[slot], buf3.at[j], sem)

    def issue(j, carry):
        row_copy(j, dest_ref[0, 0, j]).start()
        return carry

    lax.fori_loop(0, TOP_K * tm, issue, 0)

    def wait(j, carry):
        row_copy(j, 0).wait()
        return carry

    lax.fori_loop(0, TOP_K * tm, wait, 0)

    buf2[...] = buf3[...].reshape(TOP_K * tm, d)
    gate = gate_ref[0]
    acc = None
    for kk in range(TOP_K):
        term = gate[:, kk:kk + 1] * buf2[kk * tm:(kk + 1) * tm, :]
        acc = term if acc is None else acc + term
    x2 = x1_ref[0] + mod_ref[0, 5:6, :] * acc
    o_ref[0] = x2 * lax.rsqrt(jnp.mean(x2 * x2, axis=-1, keepdims=True) + EPS) * nw_ref[...]


def _combine(dest, x1, gate, mod, norm_w, ys, tm=256):
    bsz, s, d = x1.shape
    ns = s // tm
    tok = lambda n: pl.BlockSpec((1, tm, n), lambda b, i: (b, i, 0))
    return pl.pallas_call(
        _combine_kernel,
        grid=(bsz, ns),
        in_specs=[pl.BlockSpec((1, 1, TOP_K * tm), lambda b, i: (b * ns + i, 0, 0), memory_space=pltpu.SMEM),
                  tok(d),
                  pl.BlockSpec((1, tm, TOP_K), lambda b, i: (b * ns + i, 0, 0)),
                  pl.BlockSpec((1, 6, d), lambda b, i: (b, 0, 0)),
                  pl.BlockSpec((1, d), lambda b, i: (0, 0)),
                  pl.BlockSpec(memory_space=pl.ANY)],
        out_specs=tok(d),
        out_shape=jax.ShapeDtypeStruct((bsz, s, d), F32),
        scratch_shapes=[pltpu.VMEM((TOP_K * tm, 1, d), F32), pltpu.VMEM((TOP_K * tm, d), F32),
                        pltpu.SemaphoreType.DMA],
        compiler_params=_cparams(("arbitrary", "arbitrary")),
        name="combine",
    )(dest, x1, gate, mod, norm_w, ys)


def _block_diag(w):
    nb, bi, bo = w.shape
    eye = jnp.eye(nb, dtype=w.dtype)
    return (eye[:, None, :, None] * w[:, :, None, :]).reshape(nb * bi, nb * bo)


def _layer(x, c, w_ada, b_ada, norm_mix, norm_ffn, w_in, gdn_conv_w, gdn_a_log, gdn_dt_bias,
           gdn_norm_w, lru_conv_w, lru_conv_b, lru_w_a, lru_b_a, lru_w_x, lru_b_x, lru_lambda,
           w_branch_gdn, w_branch_lru, w_out, w_router, b_router, w1, b1, w2, b2, norm_final):
    bsz, s, d = x.shape
    n_tok = bsz * s
    tm = 256
    row = lambda v: v.reshape(1, -1).astype(F32)

    mod = _ada(c, w_ada, b_ada).reshape(bsz, 6, d)

    o_ba = 4 * GDN_WIDTH
    w_main = jnp.concatenate([w_in[:, :o_ba], w_in[:, o_ba + 2 * GDN_HEADS:]], axis=1).astype(BF16)
    w_ba_t = jnp.pad(w_in[:, o_ba:o_ba + 2 * GDN_HEADS].T, ((0, 128 - 2 * GDN_HEADS), (0, 0))).astype(BF16)
    qkv, z, xb, yb, gg, gl, ba, bat = _inproj(x, mod, row(norm_mix), w_main, w_ba_t)

    zeros4 = jnp.zeros((GDN_HEADS,), F32)
    hvals = jnp.stack([jnp.concatenate([zeros4, gdn_a_log.astype(F32)]),
                       jnp.concatenate([zeros4, gdn_dt_bias.astype(F32)])])
    o_gdn = _gdn(qkv, z, ba, bat, gdn_conv_w.astype(F32), hvals, hvals.T, row(gdn_norm_w))
    o_lru = _lru(xb, yb, lru_conv_w.astype(F32), row(lru_conv_b),
                 _block_diag(lru_w_a).astype(BF16), row(lru_b_a),
                 _block_diag(lru_w_x).astype(BF16), row(lru_b_x), row(lru_lambda))

    x1, h2_rows, idx_t, gate_t, rank_t, counts = _merge(
        x, o_gdn, o_lru, gg, gl, mod, row(norm_ffn),
        w_branch_gdn.astype(BF16), w_branch_lru.astype(BF16), w_out.astype(BF16),
        w_router.T.astype(F32), b_router.reshape(-1, 1).astype(F32), tm=tm)

    counts = counts.reshape(N_EXPERTS)
    padded = (counts + MOE_BLOCK - 1) // MOE_BLOCK * MOE_BLOCK
    pend = jnp.cumsum(padded)
    pstart = pend - padded
    assert (n_tok * TOP_K) % MOE_BLOCK == 0 and s % tm == 0
    n_blocks = (n_tok * TOP_K) // MOE_BLOCK + N_EXPERTS
    n_slots = n_blocks * MOE_BLOCK
    block_e = jnp.minimum(jnp.searchsorted(pend, jnp.arange(n_blocks, dtype=jnp.int32) * MOE_BLOCK,
                                           side='right'), N_EXPERTS - 1).astype(jnp.int32)
    n_used = (pend[-1:] // MOE_BLOCK).astype(jnp.int32)
    onehot = idx_t[..., None] == jnp.arange(N_EXPERTS, dtype=jnp.int32)
    dest = (rank_t + jnp.sum(jnp.where(onehot, pstart.astype(jnp.int32), 0), axis=-1)).astype(jnp.int32)
    dest = dest.reshape(n_tok // tm, 1, TOP_K * tm)
    j = jnp.arange(MOE_BLOCK, dtype=jnp.int32)
    in_seg = (j[None, :] < (padded - counts)[:, None]).reshape(-1)
    seg_slot = ((pstart + counts)[:, None] + j[None, :]).reshape(-1)
    tail_slot = pend[-1] + jnp.cumsum(~in_seg) - 1
    pad_slots = jnp.where(in_seg, seg_slot, tail_slot).astype(jnp.int32).reshape(1, 1, N_EXPERTS * MOE_BLOCK)

    xs = _dispatch(h2_rows, dest, pad_slots, n_slots, tm=tm)
    f = w1.shape[-1] // 2
    ys = _experts(block_e, n_used, xs,
                  w1[:, :, 0::2].astype(BF16), w1[:, :, 1::2].astype(BF16),
                  b1[:, None, 0::2].astype(F32), b1[:, None, 1::2].astype(F32),
                  w2.astype(BF16), b2[:, None, :].astype(F32))
    gate = jnp.swapaxes(gate_t, 1, 2)
    return _combine(dest, x1, gate, mod, row(norm_final), ys, tm=tm)


def kernel(x, c, w_ada, b_ada, norm_mix, norm_ffn, w_in, gdn_conv_w, gdn_a_log, gdn_dt_bias,
           gdn_norm_w, lru_conv_w, lru_conv_b, lru_w_a, lru_b_a, lru_w_x, lru_b_x, lru_lambda,
           w_branch_gdn, w_branch_lru, w_out, w_router, b_router, w1, b1, w2, b2, norm_final):
    assert w_ada.shape[0] == 1, "single-layer block"
    return _layer(x, c, w_ada[0], b_ada[0], norm_mix[0], norm_ffn[0], w_in[0], gdn_conv_w[0],
                  gdn_a_log[0], gdn_dt_bias[0], gdn_norm_w[0], lru_conv_w[0], lru_conv_b[0],
                  lru_w_a[0], lru_b_a[0], lru_w_x[0], lru_b_x[0], lru_lambda[0],
                  w_branch_gdn[0], w_branch_lru[0], w_out[0], w_router[0], b_router[0],
                  w1[0], b1[0], w2[0], b2[0], norm_final)
```

```python
import functools

import jax
import jax.numpy as jnp
from jax import lax
from jax.experimental import pallas as pl
from jax.experimental.pallas import tpu as pltpu

F32 = jnp.float32
BF16 = jnp.bfloat16

EPS = 1e-6
GDN_HEADS = 4
GDN_HEAD_DIM = 128
GDN_WIDTH = GDN_HEADS * GDN_HEAD_DIM
GDN_CHUNK = 64
LRU_WIDTH = 512
LRU_C = 8.0
CONV_WIDTH = 4
N_EXPERTS = 32
TOP_K = 4
SWIGLU_ALPHA = 1.702
SWIGLU_LIMIT = 7.0
MOE_BLOCK = 512
EXPERT_COLS = 256

HALO = 8
VMEM_LIMIT = 56 * 1024 * 1024
EXPERT_VMEM_LIMIT = 60 * 1024 * 1024


def _cparams(sem):
    return pltpu.CompilerParams(dimension_semantics=sem, vmem_limit_bytes=VMEM_LIMIT)


def _sigmoid(x):
    return 1.0 / (1.0 + jnp.exp(-x))


def _silu(x):
    return x * _sigmoid(x)


def _softplus(x):
    return jnp.maximum(x, 0.0) + jnp.log1p(jnp.exp(-jnp.abs(x)))


def _gelu_tanh(x):
    return 0.5 * x * (1.0 + jnp.tanh(0.7978845608028654 * (x + 0.044715 * (x * x * x))))


def _bdot(a, b):
    return jnp.dot(a.astype(BF16), b.astype(BF16), preferred_element_type=F32)


def _bdot_nt(a, b):
    return lax.dot_general(a.astype(BF16), b.astype(BF16), (((1,), (1,)), ((), ())),
                           preferred_element_type=F32)


def _bdot_tn(a, b):
    return lax.dot_general(a.astype(BF16), b.astype(BF16), (((0,), (0,)), ((), ())),
                           preferred_element_type=F32)


def _fdot(a, b):
    return jnp.dot(a, b, preferred_element_type=F32, precision=lax.Precision.HIGHEST)


def _ada_kernel(ct_ref, w_ref, b_ref, o_ref):
    ct = _silu(ct_ref[...])
    w = w_ref[...]
    for b in range(ct.shape[1]):
        o_ref[b:b + 1, :] = jnp.sum(ct[:, b:b + 1] * w, axis=0, keepdims=True) + b_ref[...]


def _ada(c, w_ada, b_ada):
    bsz, d = c.shape
    n = w_ada.shape[1]
    tn = 512
    return pl.pallas_call(
        _ada_kernel,
        grid=(n // tn,),
        in_specs=[pl.BlockSpec((d, bsz), lambda j: (0, 0)),
                  pl.BlockSpec((d, tn), lambda j: (0, j)),
                  pl.BlockSpec((1, tn), lambda j: (0, j))],
        out_specs=pl.BlockSpec((bsz, tn), lambda j: (0, j)),
        out_shape=jax.ShapeDtypeStruct((bsz, n), F32),
        compiler_params=_cparams(("arbitrary",)),
        name="ada",
    )(c.T, w_ada, b_ada.reshape(1, n))


def _inproj_kernel(x_ref, mod_ref, nw_ref, wm_ref, wba_ref,
                   qkv_ref, z_ref, xb_ref, yb_ref, gg_ref, gl_ref, ba_ref, bat_ref):
    x = x_ref[0]
    shift = mod_ref[0, 0:1, :]
    scale = mod_ref[0, 1:2, :]
    y = x * lax.rsqrt(jnp.mean(x * x, axis=-1, keepdims=True) + EPS)
    h = ((y * nw_ref[...]) * (1.0 + scale) + shift).astype(BF16)
    col = 0
    for ref in (qkv_ref, z_ref, xb_ref, yb_ref, gg_ref, gl_ref):
        n = ref.shape[-1]
        ref[0] = jnp.dot(h, wm_ref[:, col:col + n], preferred_element_type=F32).astype(ref.dtype)
        col += n
    wba = wba_ref[...]
    nt = (((1,), (1,)), ((), ()))
    bat_ref[0] = lax.dot_general(wba[0:8], h, nt, preferred_element_type=F32)
    ba_ref[0] = lax.dot_general(h, wba, nt, preferred_element_type=F32)[:, 0:8]


def _inproj(x, mod, norm_w, w_main, w_ba_t, ts=256):
    bsz, s, d = x.shape
    widths = (3 * GDN_WIDTH, GDN_WIDTH, LRU_WIDTH, LRU_WIDTH, d, d)
    tok = lambda n: pl.BlockSpec((1, ts, n), lambda b, i: (b, i, 0))
    out_shapes = [jax.ShapeDtypeStruct((bsz, s, n), BF16) for n in widths]
    out_shapes += [jax.ShapeDtypeStruct((bsz, s, 8), F32), jax.ShapeDtypeStruct((bsz, 8, s), F32)]
    out_specs = [tok(n) for n in widths]
    out_specs += [tok(8), pl.BlockSpec((1, 8, ts), lambda b, i: (b, 0, i))]
    return pl.pallas_call(
        _inproj_kernel,
        grid=(bsz, s // ts),
        in_specs=[tok(d),
                  pl.BlockSpec((1, 6, d), lambda b, i: (b, 0, 0)),
                  pl.BlockSpec((1, d), lambda b, i: (0, 0)),
                  pl.BlockSpec(w_main.shape, lambda b, i: (0, 0)),
                  pl.BlockSpec(w_ba_t.shape, lambda b, i: (0, 0))],
        out_specs=out_specs,
        out_shape=out_shapes,
        compiler_params=_cparams(("arbitrary", "arbitrary")),
        name="inproj",
    )(x, mod, norm_w, w_main, w_ba_t)


def _causal_conv(x_tile, w_ref, xbuf, first):
    ts = x_tile.shape[0]

    @pl.when(first)
    def _():
        xbuf[0:HALO, :] = jnp.zeros((HALO, xbuf.shape[1]), F32)

    xbuf[HALO:HALO + ts, :] = x_tile
    acc = None
    for j in range(CONV_WIDTH):
        off = HALO - (CONV_WIDTH - 1) + j
        term = w_ref[j:j + 1, :] * xbuf[off:off + ts, :]
        acc = term if acc is None else acc + term
    xbuf[0:HALO, :] = xbuf[ts:ts + HALO, :]
    return acc


def _gdn_kernel(qkv_ref, z_ref, ba_ref, bat_ref, cw_ref, hrow_ref, hcol_ref, nw_ref, lvl_ref,
                o_ref, xbuf, s_ref):
    ts = qkv_ref.shape[1]
    c = GDN_CHUNK
    dk = GDN_HEAD_DIM
    first = pl.program_id(1) == 0

    @pl.when(first)
    def _():
        s_ref[...] = jnp.zeros(s_ref.shape, F32)

    qkv = _silu(_causal_conv(qkv_ref[0].astype(F32), cw_ref, xbuf, first))

    ba = ba_ref[0]
    g_col = -jnp.exp(hrow_ref[0:1, :]) * _softplus(ba + hrow_ref[1:2, :])
    beta_col = _sigmoid(ba)
    r_i = lax.broadcasted_iota(jnp.int32, (ts, ts), 0)
    c_i = lax.broadcasted_iota(jnp.int32, (ts, ts), 1)
    same = (r_i // c) == (c_i // c)
    incl = same & (c_i <= r_i)
    strict = same & (c_i < r_i)
    gc_col = _fdot(jnp.where(incl, 1.0, 0.0).astype(F32), g_col)
    g_row = -jnp.exp(hcol_ref[:, 0:1]) * _softplus(bat_ref[0] + hcol_ref[:, 1:2])
    gc_row = _fdot(g_row, jnp.where(same & (r_i <= c_i), 1.0, 0.0).astype(F32))
    eye = jnp.where(r_i == c_i, 1.0, 0.0).astype(F32)
    scale = dk ** -0.5
    nw = nw_ref[...]
    n_lvl = lvl_ref.shape[0]

    heads = range(GDN_HEADS)
    ks, gccs, a_bf, t_mat, qks, rhs, q_decs = [], [], [], [], [], [], []
    for h in heads:
        q = qkv[:, h * dk:(h + 1) * dk]
        k = qkv[:, GDN_WIDTH + h * dk:GDN_WIDTH + (h + 1) * dk]
        v = qkv[:, 2 * GDN_WIDTH + h * dk:2 * GDN_WIDTH + (h + 1) * dk]
        q = q * (lax.rsqrt(jnp.sum(q * q, axis=-1, keepdims=True) + EPS) * scale)
        k = k * lax.rsqrt(jnp.sum(k * k, axis=-1, keepdims=True) + EPS)
        beta = beta_col[:, h:h + 1]
        gcc = gc_col[:, GDN_HEADS + h:GDN_HEADS + h + 1]
        gcr = gc_row[GDN_HEADS + h:GDN_HEADS + h + 1, :]
        decay = jnp.exp(jnp.where(incl, gcc - gcr, 0.0))
        kb = k * beta
        m1 = _bdot_nt(jnp.concatenate([kb, q], axis=0), k)
        a_mat = jnp.where(strict, m1[:ts] * decay, 0.0)
        e_gc = jnp.exp(gcc)
        ks.append(k)
        gccs.append(gcc)
        qks.append(jnp.where(incl, m1[ts:] * decay, 0.0))
        a_bf.append(a_mat.astype(BF16))
        t_mat.append(eye - a_mat * lvl_ref[0].astype(F32))
        rhs.append(jnp.concatenate([v * beta, kb * e_gc], axis=1))
        q_decs.append(q * e_gc)
    for lv in range(1, n_lvl):
        t_bf = [t_mat[h].astype(BF16) for h in heads]
        inner = [_bdot(a_bf[h] * lvl_ref[lv], t_bf[h]) for h in heads]
        t_mat = [t_mat[h] - _bdot(t_bf[h], inner[h]) for h in heads]
    uws = [_bdot(t_mat[h], rhs[h]) for h in heads]
    states = [s_ref[h] for h in heads]
    for n in range(ts // c):
        rows = slice(n * c, (n + 1) * c)
        for h in heads:
            gc_last = gccs[h][(n + 1) * c - 1:(n + 1) * c, :]
            k_dec = ks[h][rows] * jnp.exp(gc_last - gccs[h][rows])
            ws = _bdot(jnp.concatenate([uws[h][rows, dk:], q_decs[h][rows]], axis=0), states[h])
            v_new = uws[h][rows, :dk] - ws[:c]
            o = ws[c:] + _bdot(qks[h][rows, n * c:(n + 1) * c], v_new)
            states[h] = states[h] * jnp.exp(gc_last) + _bdot_tn(k_dec, v_new)
            zg = _silu(z_ref[0, rows, h * dk:(h + 1) * dk].astype(F32))
            o = o * lax.rsqrt(jnp.mean(o * o, axis=-1, keepdims=True) + EPS) * nw * zg
            o_ref[0, rows, h * dk:(h + 1) * dk] = o.astype(o_ref.dtype)
    for h in heads:
        s_ref[h] = states[h]


def _gdn_level_masks(ts):
    i = jnp.arange(ts)[:, None]
    j = jnp.arange(ts)[None, :]
    masks = []
    s = 1
    while s < GDN_CHUNK:
        masks.append(((i // (2 * s)) == (j // (2 * s))) & ((i // s) != (j // s)))
        s *= 2
    return jnp.stack(masks).astype(BF16)


def _gdn(qkv, z, ba, bat, conv_w, hrow, hcol, norm_w, ts=256):
    bsz, s, _ = qkv.shape
    lvl = _gdn_level_masks(ts)
    tok = lambda n: pl.BlockSpec((1, ts, n), lambda b, i: (b, i, 0))
    full = lambda a: pl.BlockSpec(a.shape, lambda b, i: (0,) * a.ndim)
    return pl.pallas_call(
        _gdn_kernel,
        grid=(bsz, s // ts),
        in_specs=[tok(3 * GDN_WIDTH), tok(GDN_WIDTH), tok(8),
                  pl.BlockSpec((1, 8, ts), lambda b, i: (b, 0, i)),
                  full(conv_w), full(hrow), full(hcol), full(norm_w), full(lvl)],
        out_specs=tok(GDN_WIDTH),
        out_shape=jax.ShapeDtypeStruct((bsz, s, GDN_WIDTH), BF16),
        scratch_shapes=[pltpu.VMEM((ts + HALO, 3 * GDN_WIDTH), F32),
                        pltpu.VMEM((GDN_HEADS, GDN_HEAD_DIM, GDN_HEAD_DIM), F32)],
        compiler_params=_cparams(("arbitrary", "arbitrary")),
        name="gdn",
    )(qkv, z, ba, bat, conv_w, hrow, hcol, norm_w, lvl)


def _lru_kernel(xb_ref, yb_ref, cw_ref, cb_ref, wa_ref, ba_ref, wx_ref, bx_ref, lam_ref,
                o_ref, xbuf, carry_ref):
    ts = xb_ref.shape[1]
    first = pl.program_id(1) == 0

    @pl.when(first)
    def _():
        carry_ref[...] = jnp.zeros(carry_ref.shape, F32)

    xc = _causal_conv(xb_ref[0].astype(F32), cw_ref, xbuf, first) + cb_ref[...]
    r = _sigmoid(_bdot(xc, wa_ref[...]) + ba_ref[...])
    i = _sigmoid(_bdot(xc, wx_ref[...]) + bx_ref[...])
    log_a = (-LRU_C) * r * _softplus(-lam_ref[...])
    a = jnp.exp(log_a)
    mult = jnp.sqrt(jnp.maximum(1.0 - jnp.exp(2.0 * log_a), 0.0))
    b = mult * (i * xc)
    row = lax.broadcasted_iota(jnp.int32, a.shape, 0)
    d = 1
    while d < ts:
        keep = row >= d
        a_s = jnp.where(keep, pltpu.roll(a, d, axis=0), 1.0)
        b_s = jnp.where(keep, pltpu.roll(b, d, axis=0), 0.0)
        b = a * b_s + b
        a = a * a_s
        d *= 2
    h = a * carry_ref[...] + b
    carry_ref[...] = h[ts - 1:ts, :]
    o_ref[0] = (h * _gelu_tanh(yb_ref[0].astype(F32))).astype(o_ref.dtype)


def _lru(xb, yb, conv_w, conv_b, wa, b_a, wx, b_x, lam, ts=256):
    bsz, s, w = xb.shape
    tok = pl.BlockSpec((1, ts, w), lambda b, i: (b, i, 0))
    full = lambda a: pl.BlockSpec(a.shape, lambda b, i: (0,) * a.ndim)
    args = (conv_w, conv_b, wa, b_a, wx, b_x, lam)
    return pl.pallas_call(
        _lru_kernel,
        grid=(bsz, s // ts),
        in_specs=[tok, tok] + [full(a) for a in args],
        out_specs=tok,
        out_shape=jax.ShapeDtypeStruct((bsz, s, w), BF16),
        scratch_shapes=[pltpu.VMEM((ts + HALO, w), F32), pltpu.VMEM((1, w), F32)],
        compiler_params=_cparams(("arbitrary", "arbitrary")),
        name="lru",
    )(xb, yb, *args)


def _merge_kernel(x_ref, og_ref, ol_ref, gg_ref, gl_ref, mod_ref, nw_ref,
                  wg_ref, wl_ref, wo_ref, wrt_ref, br_ref,
                  x1_ref, h2_ref, idx_ref, gate_ref, rank_ref, cnt_ref, carry_ref):
    tm = x_ref.shape[1]
    step = pl.program_id(0) * pl.num_programs(1) + pl.program_id(1)

    @pl.when(step == 0)
    def _():
        carry_ref[...] = jnp.zeros(carry_ref.shape, F32)

    gate_m = mod_ref[0, 2:3, :]
    shift_f = mod_ref[0, 3:4, :]
    scale_f = mod_ref[0, 4:5, :]
    merged = (_sigmoid(gg_ref[0].astype(F32)) * jnp.dot(og_ref[0], wg_ref[...], preferred_element_type=F32)
              + _sigmoid(gl_ref[0].astype(F32)) * jnp.dot(ol_ref[0], wl_ref[...], preferred_element_type=F32))
    x1 = x_ref[0] + gate_m * _bdot(merged, wo_ref[...])
    x1_ref[0] = x1
    y = x1 * lax.rsqrt(jnp.mean(x1 * x1, axis=-1, keepdims=True) + EPS)
    h2 = (y * nw_ref[...]) * (1.0 + scale_f) + shift_f
    h2_ref[...] = h2.reshape(tm, 1, h2.shape[-1])

    logits = lax.dot_general(wrt_ref[...], h2, (((1,), (1,)), ((), ())),
                             preferred_element_type=F32, precision=lax.Precision.HIGHEST) + br_ref[...]
    e_iota = lax.broadcasted_iota(jnp.int32, logits.shape, 0)
    work = logits
    sel = jnp.zeros(logits.shape, F32)
    vals, idxs = [], []
    for _ in range(TOP_K):
        m = jnp.max(work, axis=0, keepdims=True)
        idx = jnp.min(jnp.where(work == m, e_iota, N_EXPERTS), axis=0, keepdims=True)
        hit = e_iota == idx
        sel = jnp.where(hit, 1.0, sel)
        work = jnp.where(hit, -jnp.inf, work)
        vals.append(m)
        idxs.append(idx)
    exps = [jnp.exp(v - vals[0]) for v in vals]
    denom = exps[0] + exps[1] + exps[2] + exps[3]
    r_i = lax.broadcasted_iota(jnp.int32, (tm, tm), 0)
    c_i = lax.broadcasted_iota(jnp.int32, (tm, tm), 1)
    before = jnp.where(r_i < c_i, 1.0, 0.0).astype(BF16)
    prefix = jnp.dot(sel.astype(BF16), before, preferred_element_type=F32) + carry_ref[...]
    for kk in range(TOP_K):
        idx_ref[0, kk:kk + 1, :] = idxs[kk]
        gate_ref[0, kk:kk + 1, :] = exps[kk] / denom
        rank = jnp.sum(jnp.where(e_iota == idxs[kk], prefix, 0.0), axis=0, keepdims=True)
        rank_ref[0, kk:kk + 1, :] = rank.astype(jnp.int32)
    total = carry_ref[...] + jnp.sum(sel, axis=1, keepdims=True)
    carry_ref[...] = total
    cnt_ref[...] = total.astype(jnp.int32)


def _merge(x, og, ol, gg, gl, mod, norm_w, wg, wl, wo, wrt, br, tm=256):
    bsz, s, d = x.shape
    ns = s // tm
    tok = lambda n: pl.BlockSpec((1, tm, n), lambda b, i: (b, i, 0))
    full = lambda a: pl.BlockSpec(a.shape, lambda b, i: (0,) * a.ndim)
    lane = pl.BlockSpec((1, TOP_K, tm), lambda b, i: (b * ns + i, 0, 0))
    nt = bsz * ns
    return pl.pallas_call(
        _merge_kernel,
        grid=(bsz, ns),
        in_specs=[tok(d), tok(GDN_WIDTH), tok(LRU_WIDTH), tok(d), tok(d),
                  pl.BlockSpec((1, 6, d), lambda b, i: (b, 0, 0)),
                  full(norm_w), full(wg), full(wl), full(wo), full(wrt), full(br)],
        out_specs=[tok(d),
                   pl.BlockSpec((tm, 1, d), lambda b, i: (b * ns + i, 0, 0)),
                   lane, lane, lane,
                   pl.BlockSpec((N_EXPERTS, 1), lambda b, i: (0, 0))],
        out_shape=[jax.ShapeDtypeStruct((bsz, s, d), F32),
                   jax.ShapeDtypeStruct((bsz * s, 1, d), F32),
                   jax.ShapeDtypeStruct((nt, TOP_K, tm), jnp.int32),
                   jax.ShapeDtypeStruct((nt, TOP_K, tm), F32),
                   jax.ShapeDtypeStruct((nt, TOP_K, tm), jnp.int32),
                   jax.ShapeDtypeStruct((N_EXPERTS, 1), jnp.int32)],
        scratch_shapes=[pltpu.VMEM((N_EXPERTS, 1), F32)],
        compiler_params=_cparams(("arbitrary", "arbitrary")),
        name="merge",
    )(x, og, ol, gg, gl, mod, norm_w, wg, wl, wo, wrt, br)


def _dispatch_kernel(dest_ref, zinfo_ref, h_ref, xs_ref, zbuf, sem, zsem):
    tm = h_ref.shape[0]
    n_blocks = xs_ref.shape[0] // MOE_BLOCK

    def row_copy(r, slot):
        return pltpu.make_async_copy(h_ref.at[r], xs_ref.at[slot], sem)

    def issue(r, carry):
        for kk in range(TOP_K):
            row_copy(r, dest_ref[0, 0, kk * tm + r]).start()
        return carry

    lax.fori_loop(0, tm, issue, 0)

    @pl.when(pl.program_id(0) == 0)
    def _():
        zbuf[...] = jnp.zeros(zbuf.shape, F32)

        def zero_copy(start, rows):
            return pltpu.make_async_copy(zbuf.at[pl.ds(0, rows)], xs_ref.at[pl.ds(start, rows)], zsem)

        def pad_runs(wait):
            def per_expert(e, carry):
                start = zinfo_ref[0, e]
                length = zinfo_ref[0, N_EXPERTS + e]
                rows = MOE_BLOCK // 2
                while rows >= 1:
                    @pl.when((length & rows) != 0)
                    def _(rows=rows):
                        cp = zero_copy(start + (length & (-2 * rows)), rows)
                        cp.wait() if wait else cp.start()
                    rows //= 2
                return carry

            lax.fori_loop(0, N_EXPERTS, per_expert, 0)

            def per_block(b, carry):
                cp = zero_copy(b * MOE_BLOCK, MOE_BLOCK)
                cp.wait() if wait else cp.start()
                return carry

            lax.fori_loop(zinfo_ref[0, 2 * N_EXPERTS], n_blocks, per_block, 0)

        pad_runs(False)
        pad_runs(True)

    def wait(r, carry):
        for kk in range(TOP_K):
            row_copy(r, 0).wait()
        return carry

    lax.fori_loop(0, tm, wait, 0)


def _dispatch(h2_rows, dest, zinfo, n_slots, tm=256):
    n_tok, _, d = h2_rows.shape
    nt = n_tok // tm
    return pl.pallas_call(
        _dispatch_kernel,
        grid=(nt,),
        in_specs=[pl.BlockSpec((1, 1, TOP_K * tm), lambda i: (i, 0, 0), memory_space=pltpu.SMEM),
                  pl.BlockSpec(zinfo.shape, lambda i: (0, 0), memory_space=pltpu.SMEM),
                  pl.BlockSpec((tm, 1, d), lambda i: (i, 0, 0))],
        out_specs=pl.BlockSpec(memory_space=pl.ANY),
        out_shape=jax.ShapeDtypeStruct((n_slots, 1, d), F32),
        scratch_shapes=[pltpu.VMEM((MOE_BLOCK, 1, d), F32), pltpu.SemaphoreType.DMA,
                        pltpu.SemaphoreType.DMA],
        compiler_params=_cparams(("arbitrary",)),
        name="dispatch",
    )(dest, zinfo, h2_rows)


def _expert_kernel(be_ref, nu_ref, xs_ref, w1_ref, b1g_ref, b1l_ref, w2_ref, b2_ref, perm_ref,
                   ys_ref, x2d, w1p, w2b):
    blk, _, d = xs_ref.shape
    f = w2b.shape[0]
    n_groups = 2 * f // EXPERT_COLS
    half = EXPERT_COLS // 2
    b = pl.program_id(0)
    live = b < nu_ref[0]
    fresh = (b == 0) | (be_ref[b] != be_ref[jnp.maximum(b - 1, 0)])

    @pl.when(live & fresh)
    def _():
        for j in range(n_groups):
            cols = slice(j * EXPERT_COLS, (j + 1) * EXPERT_COLS)
            w1p[:, cols] = jnp.dot(w1_ref[0, :, cols].astype(BF16), perm_ref[...],
                                   preferred_element_type=F32).astype(BF16)
        w2b[...] = w2_ref[0].astype(BF16)

    @pl.when(live)
    def _():
        x2d[...] = xs_ref[...].reshape(blk, d)
        u = jnp.dot(x2d[...].astype(BF16), w1p[...], preferred_element_type=F32)
        acts = []
        for j in range(n_groups):
            u_glu = u[:, j * EXPERT_COLS:j * EXPERT_COLS + half] + b1g_ref[0, :, j * half:(j + 1) * half]
            u_lin = u[:, j * EXPERT_COLS + half:(j + 1) * EXPERT_COLS] + b1l_ref[0, :, j * half:(j + 1) * half]
            u_glu = jnp.minimum(u_glu, SWIGLU_LIMIT)
            u_lin = jnp.clip(u_lin, -SWIGLU_LIMIT, SWIGLU_LIMIT)
            acts.append((u_glu * _sigmoid(SWIGLU_ALPHA * u_glu) * (u_lin + 1.0)).astype(BF16))
        act = jnp.concatenate(acts, axis=1)
        y = jnp.dot(act, w2b[...], preferred_element_type=F32) + b2_ref[0]
        ys_ref[...] = y.reshape(blk, 1, d)

    @pl.when(jnp.logical_not(live))
    def _():
        ys_ref[...] = jnp.zeros(ys_ref.shape, F32)


def _pair_split_matrix():
    i = jnp.arange(EXPERT_COLS)[:, None]
    j = jnp.arange(EXPERT_COLS)[None, :]
    half = EXPERT_COLS // 2
    return jnp.where(j < half, i == 2 * j, i == 2 * (j - half) + 1).astype(BF16)


def _experts(block_e, n_used, xs, w1, b1g, b1l, w2, b2):
    n_slots, _, d = xs.shape
    f = w2.shape[1]
    n_blocks = n_slots // MOE_BLOCK
    perm = _pair_split_matrix()
    slot = lambda b, be, nu: (jnp.minimum(b, nu[0] - 1), 0, 0)
    wsel = lambda b, be, nu: (be[b], 0, 0)
    grid_spec = pltpu.PrefetchScalarGridSpec(
        num_scalar_prefetch=2,
        grid=(n_blocks,),
        in_specs=[pl.BlockSpec((MOE_BLOCK, 1, d), slot),
                  pl.BlockSpec((1, d, 2 * f), wsel),
                  pl.BlockSpec((1, 1, f), wsel), pl.BlockSpec((1, 1, f), wsel),
                  pl.BlockSpec((1, f, d), wsel), pl.BlockSpec((1, 1, d), wsel),
                  pl.BlockSpec(perm.shape, lambda b, be, nu: (0, 0))],
        out_specs=pl.BlockSpec((MOE_BLOCK, 1, d), lambda b, be, nu: (b, 0, 0)),
        scratch_shapes=[pltpu.VMEM((MOE_BLOCK, d), F32), pltpu.VMEM((d, 2 * f), BF16),
                        pltpu.VMEM((f, d), BF16)],
    )
    return pl.pallas_call(
        _expert_kernel,
        grid_spec=grid_spec,
        out_shape=jax.ShapeDtypeStruct((n_slots, 1, d), F32),
        compiler_params=pltpu.CompilerParams(dimension_semantics=("arbitrary",),
                                             vmem_limit_bytes=EXPERT_VMEM_LIMIT),
        name="experts",
    )(block_e, n_used, xs, w1, b1g, b1l, w2, b2, perm)


def _combine_kernel(dest_ref, dnext_ref, x1_ref, gate_ref, mod_ref, nw_ref, ys_ref, o_ref,
                    buf_a, buf_b, buf2, sems):
    tm = x1_ref.shape[1]
    d = x1_ref.shape[2]
    rows = TOP_K * tm
    step = pl.program_id(0) * pl.num_programs(1) + pl.program_id(1)
    last = pl.num_programs(0) * pl.num_programs(1) - 1

    def gather(idx_ref, buf, sem):
        def issue(j, carry):
            pltpu.make_async_copy(ys_ref.at[idx_ref[0, 0, j]], buf.at[j], sem).start()
            return carry

        lax.fori_loop(0, rows, issue, 0)

    def phase(buf, sem, nbuf, nsem):
        @pl.when(step < last)
        def _():
            gather(dnext_ref, nbuf, nsem)

        pltpu.make_async_copy(ys_ref.at[pl.ds(0, rows)], buf, sem).wait()
        buf2[...] = buf[...].reshape(rows, d)

    @pl.when(step == 0)
    def _():
        gather(dest_ref, buf_a, sems.at[0])

    @pl.when(step % 2 == 0)
    def _():
        phase(buf_a, sems.at[0], buf_b, sems.at[1])

    @pl.when(step % 2 == 1)
    def _():
        phase(buf_b, sems.at[1], buf_a, sems.at[0])

    gate = gate_ref[0]
    acc = None
    for kk in range(TOP_K):
        term = gate[:, kk:kk + 1] * buf2[kk * tm:(kk + 1) * tm, :]
        acc = term if acc is None else acc + term
    x2 = x1_ref[0] + mod_ref[0, 5:6, :] * acc
    o_ref[0] = x2 * lax.rsqrt(jnp.mean(x2 * x2, axis=-1, keepdims=True) + EPS) * nw_ref[...]


def _combine(dest, x1, gate, mod, norm_w, ys, tm=256):
    bsz, s, d = x1.shape
    ns = s // tm
    tok = lambda n: pl.BlockSpec((1, tm, n), lambda b, i: (b, i, 0))
    last = bsz * ns - 1
    idx = lambda off: pl.BlockSpec((1, 1, TOP_K * tm),
                                   lambda b, i: (jnp.minimum(b * ns + i + off, last), 0, 0),
                                   memory_space=pltpu.SMEM)
    return pl.pallas_call(
        _combine_kernel,
        grid=(bsz, ns),
        in_specs=[idx(0), idx(1),
                  tok(d),
                  pl.BlockSpec((1, tm, TOP_K), lambda b, i: (b * ns + i, 0, 0)),
                  pl.BlockSpec((1, 6, d), lambda b, i: (b, 0, 0)),
                  pl.BlockSpec((1, d), lambda b, i: (0, 0)),
                  pl.BlockSpec(memory_space=pl.ANY)],
        out_specs=tok(d),
        out_shape=jax.ShapeDtypeStruct((bsz, s, d), F32),
        scratch_shapes=[pltpu.VMEM((TOP_K * tm, 1, d), F32), pltpu.VMEM((TOP_K * tm, 1, d), F32),
                        pltpu.VMEM((TOP_K * tm, d), F32), pltpu.SemaphoreType.DMA((2,))],
        compiler_params=_cparams(("arbitrary", "arbitrary")),
        name="combine",
    )(dest, dest, x1, gate, mod, norm_w, ys)


def _block_diag(w):
    nb, bi, bo = w.shape
    eye = jnp.eye(nb, dtype=w.dtype)
    return (eye[:, None, :, None] * w[:, :, None, :]).reshape(nb * bi, nb * bo)


def _layer(x, c, w_ada, b_ada, norm_mix, norm_ffn, w_in, gdn_conv_w, gdn_a_log, gdn_dt_bias,
           gdn_norm_w, lru_conv_w, lru_conv_b, lru_w_a, lru_b_a, lru_w_x, lru_b_x, lru_lambda,
           w_branch_gdn, w_branch_lru, w_out, w_router, b_router, w1, b1, w2, b2, norm_final):
    bsz, s, d = x.shape
    n_tok = bsz * s
    tm = 256
    row = lambda v: v.reshape(1, -1).astype(F32)

    mod = _ada(c, w_ada, b_ada).reshape(bsz, 6, d)

    o_ba = 4 * GDN_WIDTH
    w_main = jnp.concatenate([w_in[:, :o_ba], w_in[:, o_ba + 2 * GDN_HEADS:]], axis=1).astype(BF16)
    w_ba_t = jnp.pad(w_in[:, o_ba:o_ba + 2 * GDN_HEADS].T, ((0, 128 - 2 * GDN_HEADS), (0, 0))).astype(BF16)
    qkv, z, xb, yb, gg, gl, ba, bat = _inproj(x, mod, row(norm_mix), w_main, w_ba_t)

    zeros4 = jnp.zeros((GDN_HEADS,), F32)
    hvals = jnp.stack([jnp.concatenate([zeros4, gdn_a_log.astype(F32)]),
                       jnp.concatenate([zeros4, gdn_dt_bias.astype(F32)])])
    o_gdn = _gdn(qkv, z, ba, bat, gdn_conv_w.astype(F32), hvals, hvals.T, row(gdn_norm_w))
    o_lru = _lru(xb, yb, lru_conv_w.astype(F32), row(lru_conv_b),
                 _block_diag(lru_w_a).astype(BF16), row(lru_b_a),
                 _block_diag(lru_w_x).astype(BF16), row(lru_b_x), row(lru_lambda))

    x1, h2_rows, idx_t, gate_t, rank_t, counts = _merge(
        x, o_gdn, o_lru, gg, gl, mod, row(norm_ffn),
        w_branch_gdn.astype(BF16), w_branch_lru.astype(BF16), w_out.astype(BF16),
        w_router.T.astype(F32), b_router.reshape(-1, 1).astype(F32), tm=tm)

    counts = counts.reshape(N_EXPERTS)
    padded = (counts + MOE_BLOCK - 1) // MOE_BLOCK * MOE_BLOCK
    pend = jnp.cumsum(padded)
    pstart = pend - padded
    assert (n_tok * TOP_K) % MOE_BLOCK == 0 and s % tm == 0
    n_blocks = (n_tok * TOP_K) // MOE_BLOCK + N_EXPERTS
    n_slots = n_blocks * MOE_BLOCK
    block_first = jnp.arange(n_blocks, dtype=jnp.int32) * MOE_BLOCK
    block_e = jnp.minimum(jnp.sum(pend[None, :] <= block_first[:, None], axis=1),
                          N_EXPERTS - 1).astype(jnp.int32)
    n_used = (pend[-1:] // MOE_BLOCK).astype(jnp.int32)
    onehot = idx_t[..., None] == jnp.arange(N_EXPERTS, dtype=jnp.int32)
    dest = (rank_t + jnp.sum(jnp.where(onehot, pstart.astype(jnp.int32), 0), axis=-1)).astype(jnp.int32)
    dest = dest.reshape(n_tok // tm, 1, TOP_K * tm)
    zinfo = jnp.concatenate([pstart + counts, padded - counts, n_used,
                             jnp.zeros((128 - 2 * N_EXPERTS - 1,), jnp.int32)]).astype(jnp.int32)

    xs = _dispatch(h2_rows, dest, zinfo.reshape(1, 128), n_slots, tm=tm)
    ys = _experts(block_e, n_used, xs, w1.astype(F32),
                  b1[:, None, 0::2].astype(F32), b1[:, None, 1::2].astype(F32),
                  w2.astype(F32), b2[:, None, :].astype(F32))
    gate = jnp.swapaxes(gate_t, 1, 2)
    return _combine(dest, x1, gate, mod, row(norm_final), ys, tm=tm)


def kernel(x, c, w_ada, b_ada, norm_mix, norm_ffn, w_in, gdn_conv_w, gdn_a_log, gdn_dt_bias,
           gdn_norm_w, lru_conv_w, lru_conv_b, lru_w_a, lru_b_a, lru_w_x, lru_b_x, lru_lambda,
           w_branch_gdn, w_branch_lru, w_out, w_router, b_router, w1, b1, w2, b2, norm_final):
    assert w_ada.shape[0] == 1, "single-layer block"
    return _layer(x, c, w_ada[0], b_ada[0], norm_mix[0], norm_ffn[0], w_in[0], gdn_conv_w[0],
                  gdn_a_log[0], gdn_dt_bias[0], gdn_norm_w[0], lru_conv_w[0], lru_conv_b[0],
                  lru_w_a[0], lru_b_a[0], lru_w_x[0], lru_b_x[0], lru_lambda[0],
                  w_branch_gdn[0], w_branch_lru[0], w_out[0], w_router[0], b_router[0],
                  w1[0], b1[0], w2[0], b2[0], norm_final)
```

```python
import functools

import jax
import jax.numpy as jnp
from jax import lax
from jax.experimental import pallas as pl
from jax.experimental.pallas import tpu as pltpu

F32 = jnp.float32
BF16 = jnp.bfloat16

EPS = 1e-6
GDN_HEADS = 4
GDN_HEAD_DIM = 128
GDN_WIDTH = GDN_HEADS * GDN_HEAD_DIM
GDN_CHUNK = 64
LRU_WIDTH = 512
LRU_C = 8.0
CONV_WIDTH = 4
N_EXPERTS = 32
TOP_K = 4
SWIGLU_ALPHA = 1.702
SWIGLU_LIMIT = 7.0
MOE_BLOCK = 512
EXPERT_COLS = 256

HALO = 8
VMEM_LIMIT = 56 * 1024 * 1024
EXPERT_VMEM_LIMIT = 60 * 1024 * 1024


def _cparams(sem):
    return pltpu.CompilerParams(dimension_semantics=sem, vmem_limit_bytes=VMEM_LIMIT)


def _sigmoid(x):
    return 1.0 / (1.0 + jnp.exp(-x))


def _silu(x):
    return x * _sigmoid(x)


def _softplus(x):
    return jnp.maximum(x, 0.0) + jnp.log1p(jnp.exp(-jnp.abs(x)))


def _gelu_tanh(x):
    return 0.5 * x * (1.0 + jnp.tanh(0.7978845608028654 * (x + 0.044715 * (x * x * x))))


def _bdot(a, b):
    return jnp.dot(a.astype(BF16), b.astype(BF16), preferred_element_type=F32)


def _bdot_nt(a, b):
    return lax.dot_general(a.astype(BF16), b.astype(BF16), (((1,), (1,)), ((), ())),
                           preferred_element_type=F32)


def _bdot_tn(a, b):
    return lax.dot_general(a.astype(BF16), b.astype(BF16), (((0,), (0,)), ((), ())),
                           preferred_element_type=F32)


def _fdot(a, b):
    return jnp.dot(a, b, preferred_element_type=F32, precision=lax.Precision.HIGHEST)


def _ada_kernel(ct_ref, w_ref, b_ref, o_ref):
    ct = _silu(ct_ref[...])
    w = w_ref[...]
    for b in range(ct.shape[1]):
        o_ref[b:b + 1, :] = jnp.sum(ct[:, b:b + 1] * w, axis=0, keepdims=True) + b_ref[...]


def _ada(c, w_ada, b_ada):
    bsz, d = c.shape
    n = w_ada.shape[1]
    tn = 512
    return pl.pallas_call(
        _ada_kernel,
        grid=(n // tn,),
        in_specs=[pl.BlockSpec((d, bsz), lambda j: (0, 0)),
                  pl.BlockSpec((d, tn), lambda j: (0, j)),
                  pl.BlockSpec((1, tn), lambda j: (0, j))],
        out_specs=pl.BlockSpec((bsz, tn), lambda j: (0, j)),
        out_shape=jax.ShapeDtypeStruct((bsz, n), F32),
        compiler_params=_cparams(("arbitrary",)),
        name="ada",
    )(c.T, w_ada, b_ada.reshape(1, n))


def _inproj_kernel(x_ref, mod_ref, nw_ref, wm_ref, wba_ref,
                   qkv_ref, z_ref, xb_ref, yb_ref, gg_ref, gl_ref, ba_ref, bat_ref):
    x = x_ref[0]
    shift = mod_ref[0, 0:1, :]
    scale = mod_ref[0, 1:2, :]
    y = x * lax.rsqrt(jnp.mean(x * x, axis=-1, keepdims=True) + EPS)
    h = ((y * nw_ref[...]) * (1.0 + scale) + shift).astype(BF16)
    col = 0
    for ref in (qkv_ref, z_ref, xb_ref, yb_ref, gg_ref, gl_ref):
        n = ref.shape[-1]
        ref[0] = jnp.dot(h, wm_ref[:, col:col + n], preferred_element_type=F32).astype(ref.dtype)
        col += n
    wba = wba_ref[...]
    nt = (((1,), (1,)), ((), ()))
    bat_ref[0] = lax.dot_general(wba[0:8], h, nt, preferred_element_type=F32)
    ba_ref[0] = lax.dot_general(h, wba, nt, preferred_element_type=F32)[:, 0:8]


def _inproj(x, mod, norm_w, w_main, w_ba_t, ts=256):
    bsz, s, d = x.shape
    widths = (3 * GDN_WIDTH, GDN_WIDTH, LRU_WIDTH, LRU_WIDTH, d, d)
    tok = lambda n: pl.BlockSpec((1, ts, n), lambda b, i: (b, i, 0))
    out_shapes = [jax.ShapeDtypeStruct((bsz, s, n), BF16) for n in widths]
    out_shapes += [jax.ShapeDtypeStruct((bsz, s, 8), F32), jax.ShapeDtypeStruct((bsz, 8, s), F32)]
    out_specs = [tok(n) for n in widths]
    out_specs += [tok(8), pl.BlockSpec((1, 8, ts), lambda b, i: (b, 0, i))]
    return pl.pallas_call(
        _inproj_kernel,
        grid=(bsz, s // ts),
        in_specs=[tok(d),
                  pl.BlockSpec((1, 6, d), lambda b, i: (b, 0, 0)),
                  pl.BlockSpec((1, d), lambda b, i: (0, 0)),
                  pl.BlockSpec(w_main.shape, lambda b, i: (0, 0)),
                  pl.BlockSpec(w_ba_t.shape, lambda b, i: (0, 0))],
        out_specs=out_specs,
        out_shape=out_shapes,
        compiler_params=_cparams(("arbitrary", "arbitrary")),
        name="inproj",
    )(x, mod, norm_w, w_main, w_ba_t)


def _causal_conv(x_tile, w_ref, xbuf, first):
    ts = x_tile.shape[0]

    @pl.when(first)
    def _():
        xbuf[0:HALO, :] = jnp.zeros((HALO, xbuf.shape[1]), F32)

    xbuf[HALO:HALO + ts, :] = x_tile
    acc = None
    for j in range(CONV_WIDTH):
        off = HALO - (CONV_WIDTH - 1) + j
        term = w_ref[j:j + 1, :] * xbuf[off:off + ts, :]
        acc = term if acc is None else acc + term
    xbuf[0:HALO, :] = xbuf[ts:ts + HALO, :]
    return acc


def _gdn_kernel(qkv_ref, z_ref, ba_ref, bat_ref, cw_ref, hrow_ref, hcol_ref, nw_ref, lvl_ref,
                o_ref, xbuf, s_ref):
    ts = qkv_ref.shape[1]
    c = GDN_CHUNK
    dk = GDN_HEAD_DIM
    first = pl.program_id(1) == 0

    @pl.when(first)
    def _():
        s_ref[...] = jnp.zeros(s_ref.shape, F32)

    qkv = _silu(_causal_conv(qkv_ref[0].astype(F32), cw_ref, xbuf, first))

    ba = ba_ref[0]
    g_col = -jnp.exp(hrow_ref[0:1, :]) * _softplus(ba + hrow_ref[1:2, :])
    beta_col = _sigmoid(ba)
    r_i = lax.broadcasted_iota(jnp.int32, (ts, ts), 0)
    c_i = lax.broadcasted_iota(jnp.int32, (ts, ts), 1)
    same = (r_i // c) == (c_i // c)
    incl = same & (c_i <= r_i)
    strict = same & (c_i < r_i)
    gc_col = _fdot(jnp.where(incl, 1.0, 0.0).astype(F32), g_col)
    g_row = -jnp.exp(hcol_ref[:, 0:1]) * _softplus(bat_ref[0] + hcol_ref[:, 1:2])
    gc_row = _fdot(g_row, jnp.where(same & (r_i <= c_i), 1.0, 0.0).astype(F32))
    eye = jnp.where(r_i == c_i, 1.0, 0.0).astype(F32)
    scale = dk ** -0.5
    nw = nw_ref[...]
    n_lvl = lvl_ref.shape[0]

    heads = range(GDN_HEADS)
    ks, gccs, a_bf, t_mat, qks, rhs, q_decs = [], [], [], [], [], [], []
    for h in heads:
        q = qkv[:, h * dk:(h + 1) * dk]
        k = qkv[:, GDN_WIDTH + h * dk:GDN_WIDTH + (h + 1) * dk]
        v = qkv[:, 2 * GDN_WIDTH + h * dk:2 * GDN_WIDTH + (h + 1) * dk]
        q = q * (lax.rsqrt(jnp.sum(q * q, axis=-1, keepdims=True) + EPS) * scale)
        k = k * lax.rsqrt(jnp.sum(k * k, axis=-1, keepdims=True) + EPS)
        beta = beta_col[:, h:h + 1]
        gcc = gc_col[:, GDN_HEADS + h:GDN_HEADS + h + 1]
        gcr = gc_row[GDN_HEADS + h:GDN_HEADS + h + 1, :]
        decay = jnp.exp(jnp.where(incl, gcc - gcr, 0.0))
        kb = k * beta
        m1 = _bdot_nt(jnp.concatenate([kb, q], axis=0), k)
        a_mat = jnp.where(strict, m1[:ts] * decay, 0.0)
        e_gc = jnp.exp(gcc)
        ks.append(k)
        gccs.append(gcc)
        qks.append(jnp.where(incl, m1[ts:] * decay, 0.0))
        a_bf.append(a_mat.astype(BF16))
        t_mat.append(eye - a_mat * lvl_ref[0].astype(F32))
        rhs.append(jnp.concatenate([v * beta, kb * e_gc], axis=1))
        q_decs.append(q * e_gc)
    for lv in range(1, n_lvl):
        t_bf = [t_mat[h].astype(BF16) for h in heads]
        inner = [_bdot(a_bf[h] * lvl_ref[lv], t_bf[h]) for h in heads]
        t_mat = [t_mat[h] - _bdot(t_bf[h], inner[h]) for h in heads]
    uws = [_bdot(t_mat[h], rhs[h]) for h in heads]
    states = [s_ref[h] for h in heads]
    for n in range(ts // c):
        rows = slice(n * c, (n + 1) * c)
        for h in heads:
            gc_last = gccs[h][(n + 1) * c - 1:(n + 1) * c, :]
            k_dec = ks[h][rows] * jnp.exp(gc_last - gccs[h][rows])
            ws = _bdot(jnp.concatenate([uws[h][rows, dk:], q_decs[h][rows]], axis=0), states[h])
            v_new = uws[h][rows, :dk] - ws[:c]
            o = ws[c:] + _bdot(qks[h][rows, n * c:(n + 1) * c], v_new)
            states[h] = states[h] * jnp.exp(gc_last) + _bdot_tn(k_dec, v_new)
            zg = _silu(z_ref[0, rows, h * dk:(h + 1) * dk].astype(F32))
            o = o * lax.rsqrt(jnp.mean(o * o, axis=-1, keepdims=True) + EPS) * nw * zg
            o_ref[0, rows, h * dk:(h + 1) * dk] = o.astype(o_ref.dtype)
    for h in heads:
        s_ref[h] = states[h]


def _gdn_level_masks(ts):
    i = jnp.arange(ts)[:, None]
    j = jnp.arange(ts)[None, :]
    masks = []
    s = 1
    while s < GDN_CHUNK:
        masks.append(((i // (2 * s)) == (j // (2 * s))) & ((i // s) != (j // s)))
        s *= 2
    return jnp.stack(masks).astype(BF16)


def _gdn(qkv, z, ba, bat, conv_w, hrow, hcol, norm_w, ts=256):
    bsz, s, _ = qkv.shape
    lvl = _gdn_level_masks(ts)
    tok = lambda n: pl.BlockSpec((1, ts, n), lambda b, i: (b, i, 0))
    full = lambda a: pl.BlockSpec(a.shape, lambda b, i: (0,) * a.ndim)
    return pl.pallas_call(
        _gdn_kernel,
        grid=(bsz, s // ts),
        in_specs=[tok(3 * GDN_WIDTH), tok(GDN_WIDTH), tok(8),
                  pl.BlockSpec((1, 8, ts), lambda b, i: (b, 0, i)),
                  full(conv_w), full(hrow), full(hcol), full(norm_w), full(lvl)],
        out_specs=tok(GDN_WIDTH),
        out_shape=jax.ShapeDtypeStruct((bsz, s, GDN_WIDTH), BF16),
        scratch_shapes=[pltpu.VMEM((ts + HALO, 3 * GDN_WIDTH), F32),
                        pltpu.VMEM((GDN_HEADS, GDN_HEAD_DIM, GDN_HEAD_DIM), F32)],
        compiler_params=_cparams(("arbitrary", "arbitrary")),
        name="gdn",
    )(qkv, z, ba, bat, conv_w, hrow, hcol, norm_w, lvl)


def _lru_kernel(xb_ref, yb_ref, cw_ref, cb_ref, wa_ref, ba_ref, wx_ref, bx_ref, lam_ref,
                o_ref, xbuf, carry_ref):
    ts = xb_ref.shape[1]
    first = pl.program_id(1) == 0

    @pl.when(first)
    def _():
        carry_ref[...] = jnp.zeros(carry_ref.shape, F32)

    xc = _causal_conv(xb_ref[0].astype(F32), cw_ref, xbuf, first) + cb_ref[...]
    r = _sigmoid(_bdot(xc, wa_ref[...]) + ba_ref[...])
    i = _sigmoid(_bdot(xc, wx_ref[...]) + bx_ref[...])
    log_a = (-LRU_C) * r * _softplus(-lam_ref[...])
    a = jnp.exp(log_a)
    mult = jnp.sqrt(jnp.maximum(1.0 - jnp.exp(2.0 * log_a), 0.0))
    b = mult * (i * xc)
    row = lax.broadcasted_iota(jnp.int32, a.shape, 0)
    d = 1
    while d < ts:
        keep = row >= d
        a_s = jnp.where(keep, pltpu.roll(a, d, axis=0), 1.0)
        b_s = jnp.where(keep, pltpu.roll(b, d, axis=0), 0.0)
        b = a * b_s + b
        a = a * a_s
        d *= 2
    h = a * carry_ref[...] + b
    carry_ref[...] = h[ts - 1:ts, :]
    o_ref[0] = (h * _gelu_tanh(yb_ref[0].astype(F32))).astype(o_ref.dtype)


def _lru(xb, yb, conv_w, conv_b, wa, b_a, wx, b_x, lam, ts=256):
    bsz, s, w = xb.shape
    tok = pl.BlockSpec((1, ts, w), lambda b, i: (b, i, 0))
    full = lambda a: pl.BlockSpec(a.shape, lambda b, i: (0,) * a.ndim)
    args = (conv_w, conv_b, wa, b_a, wx, b_x, lam)
    return pl.pallas_call(
        _lru_kernel,
        grid=(bsz, s // ts),
        in_specs=[tok, tok] + [full(a) for a in args],
        out_specs=tok,
        out_shape=jax.ShapeDtypeStruct((bsz, s, w), BF16),
        scratch_shapes=[pltpu.VMEM((ts + HALO, w), F32), pltpu.VMEM((1, w), F32)],
        compiler_params=_cparams(("arbitrary", "arbitrary")),
        name="lru",
    )(xb, yb, *args)


def _merge_kernel(x_ref, og_ref, ol_ref, gg_ref, gl_ref, mod_ref, nw_ref,
                  wg_ref, wl_ref, wo_ref, wrt_ref, br_ref,
                  x1_ref, h2_ref, idx_ref, gate_ref, rank_ref, cnt_ref, carry_ref):
    tm = x_ref.shape[1]
    step = pl.program_id(0) * pl.num_programs(1) + pl.program_id(1)

    @pl.when(step == 0)
    def _():
        carry_ref[...] = jnp.zeros(carry_ref.shape, F32)

    gate_m = mod_ref[0, 2:3, :]
    shift_f = mod_ref[0, 3:4, :]
    scale_f = mod_ref[0, 4:5, :]
    merged = (_sigmoid(gg_ref[0].astype(F32)) * jnp.dot(og_ref[0], wg_ref[...], preferred_element_type=F32)
              + _sigmoid(gl_ref[0].astype(F32)) * jnp.dot(ol_ref[0], wl_ref[...], preferred_element_type=F32))
    x1 = x_ref[0] + gate_m * _bdot(merged, wo_ref[...])
    x1_ref[0] = x1
    y = x1 * lax.rsqrt(jnp.mean(x1 * x1, axis=-1, keepdims=True) + EPS)
    h2 = (y * nw_ref[...]) * (1.0 + scale_f) + shift_f
    h2_ref[...] = h2.reshape(tm, 1, h2.shape[-1])

    logits = lax.dot_general(wrt_ref[...], h2, (((1,), (1,)), ((), ())),
                             preferred_element_type=F32, precision=lax.Precision.HIGHEST) + br_ref[...]
    e_iota = lax.broadcasted_iota(jnp.int32, logits.shape, 0)
    work = logits
    sel = jnp.zeros(logits.shape, F32)
    vals, idxs = [], []
    for _ in range(TOP_K):
        m = jnp.max(work, axis=0, keepdims=True)
        idx = jnp.min(jnp.where(work == m, e_iota, N_EXPERTS), axis=0, keepdims=True)
        hit = e_iota == idx
        sel = jnp.where(hit, 1.0, sel)
        work = jnp.where(hit, -jnp.inf, work)
        vals.append(m)
        idxs.append(idx)
    exps = [jnp.exp(v - vals[0]) for v in vals]
    denom = exps[0] + exps[1] + exps[2] + exps[3]
    r_i = lax.broadcasted_iota(jnp.int32, (tm, tm), 0)
    c_i = lax.broadcasted_iota(jnp.int32, (tm, tm), 1)
    before = jnp.where(r_i < c_i, 1.0, 0.0).astype(BF16)
    prefix = jnp.dot(sel.astype(BF16), before, preferred_element_type=F32) + carry_ref[...]
    for kk in range(TOP_K):
        idx_ref[0, kk:kk + 1, :] = idxs[kk]
        gate_ref[0, kk:kk + 1, :] = exps[kk] / denom
        rank = jnp.sum(jnp.where(e_iota == idxs[kk], prefix, 0.0), axis=0, keepdims=True)
        rank_ref[0, kk:kk + 1, :] = rank.astype(jnp.int32)
    total = carry_ref[...] + jnp.sum(sel, axis=1, keepdims=True)
    carry_ref[...] = total
    cnt_ref[...] = total.astype(jnp.int32)


def _merge(x, og, ol, gg, gl, mod, norm_w, wg, wl, wo, wrt, br, tm=256):
    bsz, s, d = x.shape
    ns = s // tm
    tok = lambda n: pl.BlockSpec((1, tm, n), lambda b, i: (b, i, 0))
    full = lambda a: pl.BlockSpec(a.shape, lambda b, i: (0,) * a.ndim)
    lane = pl.BlockSpec((1, TOP_K, tm), lambda b, i: (b * ns + i, 0, 0))
    nt = bsz * ns
    return pl.pallas_call(
        _merge_kernel,
        grid=(bsz, ns),
        in_specs=[tok(d), tok(GDN_WIDTH), tok(LRU_WIDTH), tok(d), tok(d),
                  pl.BlockSpec((1, 6, d), lambda b, i: (b, 0, 0)),
                  full(norm_w), full(wg), full(wl), full(wo), full(wrt), full(br)],
        out_specs=[tok(d),
                   pl.BlockSpec((tm, 1, d), lambda b, i: (b * ns + i, 0, 0)),
                   lane, lane, lane,
                   pl.BlockSpec((N_EXPERTS, 1), lambda b, i: (0, 0))],
        out_shape=[jax.ShapeDtypeStruct((bsz, s, d), F32),
                   jax.ShapeDtypeStruct((bsz * s, 1, d), F32),
                   jax.ShapeDtypeStruct((nt, TOP_K, tm), jnp.int32),
                   jax.ShapeDtypeStruct((nt, TOP_K, tm), F32),
                   jax.ShapeDtypeStruct((nt, TOP_K, tm), jnp.int32),
                   jax.ShapeDtypeStruct((N_EXPERTS, 1), jnp.int32)],
        scratch_shapes=[pltpu.VMEM((N_EXPERTS, 1), F32)],
        compiler_params=_cparams(("arbitrary", "arbitrary")),
        name="merge",
    )(x, og, ol, gg, gl, mod, norm_w, wg, wl, wo, wrt, br)


def _dispatch_kernel(dest_ref, zinfo_ref, h_ref, xs_ref, zbuf, sem, zsem):
    tm = h_ref.shape[0]
    n_blocks = xs_ref.shape[0] // MOE_BLOCK

    def row_copy(r, slot):
        return pltpu.make_async_copy(h_ref.at[r], xs_ref.at[slot], sem)

    def issue(r, carry):
        for kk in range(TOP_K):
            row_copy(r, dest_ref[0, 0, kk * tm + r]).start()
        return carry

    lax.fori_loop(0, tm, issue, 0)

    @pl.when(pl.program_id(0) == 0)
    def _():
        zbuf[...] = jnp.zeros(zbuf.shape, F32)

        def zero_copy(start, rows):
            return pltpu.make_async_copy(zbuf.at[pl.ds(0, rows)], xs_ref.at[pl.ds(start, rows)], zsem)

        def pad_runs(wait):
            def per_expert(e, carry):
                start = zinfo_ref[0, e]
                length = zinfo_ref[0, N_EXPERTS + e]
                rows = MOE_BLOCK // 2
                while rows >= 1:
                    @pl.when((length & rows) != 0)
                    def _(rows=rows):
                        cp = zero_copy(start + (length & (-2 * rows)), rows)
                        cp.wait() if wait else cp.start()
                    rows //= 2
                return carry

            lax.fori_loop(0, N_EXPERTS, per_expert, 0)

            def per_block(b, carry):
                cp = zero_copy(b * MOE_BLOCK, MOE_BLOCK)
                cp.wait() if wait else cp.start()
                return carry

            lax.fori_loop(zinfo_ref[0, 2 * N_EXPERTS], n_blocks, per_block, 0)

        pad_runs(False)
        pad_runs(True)

    def wait(r, carry):
        for kk in range(TOP_K):
            row_copy(r, 0).wait()
        return carry

    lax.fori_loop(0, tm, wait, 0)


def _dispatch(h2_rows, dest, zinfo, n_slots, tm=256):
    n_tok, _, d = h2_rows.shape
    nt = n_tok // tm
    return pl.pallas_call(
        _dispatch_kernel,
        grid=(nt,),
        in_specs=[pl.BlockSpec((1, 1, TOP_K * tm), lambda i: (i, 0, 0), memory_space=pltpu.SMEM),
                  pl.BlockSpec(zinfo.shape, lambda i: (0, 0), memory_space=pltpu.SMEM),
                  pl.BlockSpec((tm, 1, d), lambda i: (i, 0, 0))],
        out_specs=pl.BlockSpec(memory_space=pl.ANY),
        out_shape=jax.ShapeDtypeStruct((n_slots, 1, d), F32),
        scratch_shapes=[pltpu.VMEM((MOE_BLOCK, 1, d), F32), pltpu.SemaphoreType.DMA,
                        pltpu.SemaphoreType.DMA],
        compiler_params=_cparams(("arbitrary",)),
        name="dispatch",
    )(dest, zinfo, h2_rows)


def _expert_kernel(be_ref, nu_ref, src_ref, xs_ref, w1_ref, b1g_ref, b1l_ref, w2_ref, b2_ref, perm_ref,
                   yt_ref, x2d, w1p, w2b, yb0, yb1, yb2, sems):
    blk, _, d = xs_ref.shape
    f = w2b.shape[0]
    n_groups = 2 * f // EXPERT_COLS
    half = EXPERT_COLS // 2
    b = pl.program_id(0)
    n_blocks = pl.num_programs(0) - 1
    bufs = (yb0, yb1, yb2)
    live = b < nu_ref[0]
    b_w = jnp.minimum(b, n_blocks - 1)
    fresh = (b == 0) | (be_ref[b_w] != be_ref[jnp.maximum(b_w - 1, 0)])

    def row_copy(buf, sem, j, dst):
        return pltpu.make_async_copy(buf.at[j], yt_ref.at[dst], sem)

    def tile_wait(buf, sem):
        pltpu.make_async_copy(buf, yt_ref.at[pl.ds(0, blk)], sem).wait()

    def ffn(out_buf):
        x2d[...] = xs_ref[...].reshape(blk, d)
        u = jnp.dot(x2d[...].astype(BF16), w1p[...], preferred_element_type=F32)
        acts = []
        for j in range(n_groups):
            u_glu = u[:, j * EXPERT_COLS:j * EXPERT_COLS + half] + b1g_ref[0, :, j * half:(j + 1) * half]
            u_lin = u[:, j * EXPERT_COLS + half:(j + 1) * EXPERT_COLS] + b1l_ref[0, :, j * half:(j + 1) * half]
            u_glu = jnp.minimum(u_glu, SWIGLU_LIMIT)
            u_lin = jnp.clip(u_lin, -SWIGLU_LIMIT, SWIGLU_LIMIT)
            acts.append((u_glu * _sigmoid(SWIGLU_ALPHA * u_glu) * (u_lin + 1.0)).astype(BF16))
        act = jnp.concatenate(acts, axis=1)
        y = jnp.dot(act, w2b[...], preferred_element_type=F32) + b2_ref[0]
        out_buf[...] = y.reshape(blk, 1, d)

    @pl.when(live & fresh)
    def _():
        for j in range(n_groups):
            cols = slice(j * EXPERT_COLS, (j + 1) * EXPERT_COLS)
            w1p[:, cols] = jnp.dot(w1_ref[0, :, cols].astype(BF16), perm_ref[...],
                                   preferred_element_type=F32).astype(BF16)
        w2b[...] = w2_ref[0].astype(BF16)

    for p in range(3):
        cur, prv = bufs[p], bufs[(p + 2) % 3]
        sem_cur, sem_prv, sem_old = sems.at[p], sems.at[(p + 2) % 3], sems.at[(p + 1) % 3]
        mine = b % 3 == p

        @pl.when(mine & (b >= 3))
        def _(cur=cur, sem_cur=sem_cur):
            tile_wait(cur, sem_cur)

        @pl.when(mine & (b >= 1) & live)
        def _(cur=cur, prv=prv, sem_prv=sem_prv):
            for j in range(blk):
                row_copy(prv, sem_prv, j, src_ref[0, 0, j]).start()
            ffn(cur)

        @pl.when(mine & ((b == 0) | jnp.logical_not(live)))
        def _(cur=cur, prv=prv, old=bufs[(p + 1) % 3], sem_prv=sem_prv, sem_old=sem_old):
            @pl.when(b >= 1)
            def _():
                def issue(j, carry):
                    row_copy(prv, sem_prv, j, src_ref[0, 0, j]).start()
                    return carry

                lax.fori_loop(0, blk, issue, 0)

            @pl.when(live)
            def _():
                ffn(cur)

            @pl.when(jnp.logical_not(live) & (b < n_blocks))
            def _():
                cur[...] = jnp.zeros(cur.shape, F32)

            @pl.when(b == n_blocks)
            def _():
                tile_wait(old, sem_old)
                tile_wait(prv, sem_prv)


def _pair_split_matrix():
    i = jnp.arange(EXPERT_COLS)[:, None]
    j = jnp.arange(EXPERT_COLS)[None, :]
    half = EXPERT_COLS // 2
    return jnp.where(j < half, i == 2 * j, i == 2 * (j - half) + 1).astype(BF16)


def _experts(block_e, n_used, slot_src, xs, w1, b1g, b1l, w2, b2, n_rows_out):
    n_slots, _, d = xs.shape
    f = w2.shape[1]
    n_blocks = n_slots // MOE_BLOCK
    perm = _pair_split_matrix()
    slot = lambda b, be, nu: (jnp.minimum(b, nu[0] - 1), 0, 0)
    wsel = lambda b, be, nu: (be[jnp.minimum(b, n_blocks - 1)], 0, 0)
    row_buf = pltpu.VMEM((MOE_BLOCK, 1, d), F32)
    grid_spec = pltpu.PrefetchScalarGridSpec(
        num_scalar_prefetch=2,
        grid=(n_blocks + 1,),
        in_specs=[pl.BlockSpec((1, 1, MOE_BLOCK), lambda b, be, nu: (jnp.maximum(b - 1, 0), 0, 0),
                               memory_space=pltpu.SMEM),
                  pl.BlockSpec((MOE_BLOCK, 1, d), slot),
                  pl.BlockSpec((1, d, 2 * f), wsel),
                  pl.BlockSpec((1, 1, f), wsel), pl.BlockSpec((1, 1, f), wsel),
                  pl.BlockSpec((1, f, d), wsel), pl.BlockSpec((1, 1, d), wsel),
                  pl.BlockSpec(perm.shape, lambda b, be, nu: (0, 0))],
        out_specs=pl.BlockSpec(memory_space=pl.ANY),
        scratch_shapes=[pltpu.VMEM((MOE_BLOCK, d), F32), pltpu.VMEM((d, 2 * f), BF16),
                        pltpu.VMEM((f, d), BF16), row_buf, row_buf, row_buf,
                        pltpu.SemaphoreType.DMA((3,))],
    )
    return pl.pallas_call(
        _expert_kernel,
        grid_spec=grid_spec,
        out_shape=jax.ShapeDtypeStruct((n_rows_out, 1, d), F32),
        compiler_params=pltpu.CompilerParams(dimension_semantics=("arbitrary",),
                                             vmem_limit_bytes=EXPERT_VMEM_LIMIT),
        name="experts",
    )(block_e, n_used, slot_src, xs, w1, b1g, b1l, w2, b2, perm)


def _combine_kernel(x1_ref, gate_ref, mod_ref, nw_ref, yt_ref, o_ref, buf2):
    tm = x1_ref.shape[1]
    d = x1_ref.shape[2]
    buf2[...] = yt_ref[...].reshape(TOP_K * tm, d)
    gate = gate_ref[0]
    acc = None
    for kk in range(TOP_K):
        term = gate[:, kk:kk + 1] * buf2[kk * tm:(kk + 1) * tm, :]
        acc = term if acc is None else acc + term
    x2 = x1_ref[0] + mod_ref[0, 5:6, :] * acc
    o_ref[0] = x2 * lax.rsqrt(jnp.mean(x2 * x2, axis=-1, keepdims=True) + EPS) * nw_ref[...]


def _combine(x1, gate, mod, norm_w, yt, tm=256):
    bsz, s, d = x1.shape
    ns = s // tm
    tok = lambda n: pl.BlockSpec((1, tm, n), lambda b, i: (b, i, 0))
    return pl.pallas_call(
        _combine_kernel,
        grid=(bsz, ns),
        in_specs=[tok(d),
                  pl.BlockSpec((1, tm, TOP_K), lambda b, i: (b * ns + i, 0, 0)),
                  pl.BlockSpec((1, 6, d), lambda b, i: (b, 0, 0)),
                  pl.BlockSpec((1, d), lambda b, i: (0, 0)),
                  pl.BlockSpec((TOP_K * tm, 1, d), lambda b, i: (b * ns + i, 0, 0))],
        out_specs=tok(d),
        out_shape=jax.ShapeDtypeStruct((bsz, s, d), F32),
        scratch_shapes=[pltpu.VMEM((TOP_K * tm, d), F32)],
        compiler_params=_cparams(("arbitrary", "arbitrary")),
        name="combine",
    )(x1, gate, mod, norm_w, yt)


def _block_diag(w):
    nb, bi, bo = w.shape
    eye = jnp.eye(nb, dtype=w.dtype)
    return (eye[:, None, :, None] * w[:, :, None, :]).reshape(nb * bi, nb * bo)


def _layer(x, c, w_ada, b_ada, norm_mix, norm_ffn, w_in, gdn_conv_w, gdn_a_log, gdn_dt_bias,
           gdn_norm_w, lru_conv_w, lru_conv_b, lru_w_a, lru_b_a, lru_w_x, lru_b_x, lru_lambda,
           w_branch_gdn, w_branch_lru, w_out, w_router, b_router, w1, b1, w2, b2, norm_final):
    bsz, s, d = x.shape
    n_tok = bsz * s
    tm = 256
    row = lambda v: v.reshape(1, -1).astype(F32)

    mod = _ada(c, w_ada, b_ada).reshape(bsz, 6, d)

    o_ba = 4 * GDN_WIDTH
    w_main = jnp.concatenate([w_in[:, :o_ba], w_in[:, o_ba + 2 * GDN_HEADS:]], axis=1).astype(BF16)
    w_ba_t = jnp.pad(w_in[:, o_ba:o_ba + 2 * GDN_HEADS].T, ((0, 128 - 2 * GDN_HEADS), (0, 0))).astype(BF16)
    qkv, z, xb, yb, gg, gl, ba, bat = _inproj(x, mod, row(norm_mix), w_main, w_ba_t)

    zeros4 = jnp.zeros((GDN_HEADS,), F32)
    hvals = jnp.stack([jnp.concatenate([zeros4, gdn_a_log.astype(F32)]),
                       jnp.concatenate([zeros4, gdn_dt_bias.astype(F32)])])
    o_gdn = _gdn(qkv, z, ba, bat, gdn_conv_w.astype(F32), hvals, hvals.T, row(gdn_norm_w))
    o_lru = _lru(xb, yb, lru_conv_w.astype(F32), row(lru_conv_b),
                 _block_diag(lru_w_a).astype(BF16), row(lru_b_a),
                 _block_diag(lru_w_x).astype(BF16), row(lru_b_x), row(lru_lambda))

    x1, h2_rows, idx_t, gate_t, rank_t, counts = _merge(
        x, o_gdn, o_lru, gg, gl, mod, row(norm_ffn),
        w_branch_gdn.astype(BF16), w_branch_lru.astype(BF16), w_out.astype(BF16),
        w_router.T.astype(F32), b_router.reshape(-1, 1).astype(F32), tm=tm)

    counts = counts.reshape(N_EXPERTS)
    padded = (counts + MOE_BLOCK - 1) // MOE_BLOCK * MOE_BLOCK
    pend = jnp.cumsum(padded)
    pstart = pend - padded
    assert (n_tok * TOP_K) % MOE_BLOCK == 0 and s % tm == 0
    n_blocks = (n_tok * TOP_K) // MOE_BLOCK + N_EXPERTS
    n_slots = n_blocks * MOE_BLOCK
    block_first = jnp.arange(n_blocks, dtype=jnp.int32) * MOE_BLOCK
    block_e = jnp.minimum(jnp.sum(pend[None, :] <= block_first[:, None], axis=1),
                          N_EXPERTS - 1).astype(jnp.int32)
    n_used = (pend[-1:] // MOE_BLOCK).astype(jnp.int32)
    onehot = idx_t[..., None] == jnp.arange(N_EXPERTS, dtype=jnp.int32)
    dest = (rank_t + jnp.sum(jnp.where(onehot, pstart.astype(jnp.int32), 0), axis=-1)).astype(jnp.int32)
    dest = dest.reshape(n_tok // tm, 1, TOP_K * tm)
    zinfo = jnp.concatenate([pstart + counts, padded - counts, n_used,
                             jnp.zeros((128 - 2 * N_EXPERTS - 1,), jnp.int32)]).astype(jnp.int32)

    n_assign = n_tok * TOP_K
    _, src_sorted = lax.sort_key_val(dest.reshape(-1), jnp.arange(n_assign, dtype=jnp.int32))
    src_sorted = jnp.concatenate([src_sorted, jnp.zeros((MOE_BLOCK,), jnp.int32)])
    cstart = (jnp.cumsum(counts) - counts).astype(jnp.int32)
    seg_rank = block_first - pstart[block_e].astype(jnp.int32)
    n_valid = jnp.clip(counts[block_e].astype(jnp.int32) - seg_rank, 0, MOE_BLOCK)
    win_start = jnp.clip(cstart[block_e] + seg_rank, 0, n_assign)
    window = jax.vmap(lambda st: lax.dynamic_slice(src_sorted, (st,), (MOE_BLOCK,)))(win_start)
    j = jnp.arange(MOE_BLOCK, dtype=jnp.int32)
    spare = n_assign + (block_first - cstart[block_e] - counts[block_e].astype(jnp.int32))[:, None] + j[None, :]
    slot_src = jnp.where(j[None, :] < n_valid[:, None], window, spare).astype(jnp.int32)
    slot_src = slot_src.reshape(n_blocks, 1, MOE_BLOCK)

    xs = _dispatch(h2_rows, dest, zinfo.reshape(1, 128), n_slots, tm=tm)
    yt = _experts(block_e, n_used, slot_src, xs, w1.astype(F32),
                  b1[:, None, 0::2].astype(F32), b1[:, None, 1::2].astype(F32),
                  w2.astype(F32), b2[:, None, :].astype(F32), n_assign + N_EXPERTS * MOE_BLOCK)
    gate = jnp.swapaxes(gate_t, 1, 2)
    return _combine(x1, gate, mod, row(norm_final), yt, tm=tm)


def kernel(x, c, w_ada, b_ada, norm_mix, norm_ffn, w_in, gdn_conv_w, gdn_a_log, gdn_dt_bias,
           gdn_norm_w, lru_conv_w, lru_conv_b, lru_w_a, lru_b_a, lru_w_x, lru_b_x, lru_lambda,
           w_branch_gdn, w_branch_lru, w_out, w_router, b_router, w1, b1, w2, b2, norm_final):
    assert w_ada.shape[0] == 1, "single-layer block"
    return _layer(x, c, w_ada[0], b_ada[0], norm_mix[0], norm_ffn[0], w_in[0], gdn_conv_w[0],
                  gdn_a_log[0], gdn_dt_bias[0], gdn_norm_w[0], lru_conv_w[0], lru_conv_b[0],
                  lru_w_a[0], lru_b_a[0], lru_w_x[0], lru_b_x[0], lru_lambda[0],
                  w_branch_gdn[0], w_branch_lru[0], w_out[0], w_router[0], b_router[0],
                  w1[0], b1[0], w2[0], b2[0], norm_final)
```

```python
import functools

import jax
import jax.numpy as jnp
from jax import lax
from jax.experimental import pallas as pl
from jax.experimental.pallas import tpu as pltpu

F32 = jnp.float32
BF16 = jnp.bfloat16

EPS = 1e-6
GDN_HEADS = 4
GDN_HEAD_DIM = 128
GDN_WIDTH = GDN_HEADS * GDN_HEAD_DIM
GDN_CHUNK = 64
LRU_WIDTH = 512
LRU_C = 8.0
LRU_GROUP = 128
CONV_WIDTH = 4
N_EXPERTS = 32
TOP_K = 4
SWIGLU_ALPHA = 1.702
SWIGLU_LIMIT = 7.0
MOE_BLOCK = 512
EXPERT_COLS = 256

HALO = 8
VMEM_LIMIT = 56 * 1024 * 1024
EXPERT_VMEM_LIMIT = 60 * 1024 * 1024


def _cparams(sem):
    return pltpu.CompilerParams(dimension_semantics=sem, vmem_limit_bytes=VMEM_LIMIT)


def _sigmoid(x):
    return 1.0 / (1.0 + jnp.exp(-x))


def _silu(x):
    return x * _sigmoid(x)


def _softplus(x):
    return jnp.maximum(x, 0.0) + jnp.log1p(jnp.exp(-jnp.abs(x)))


def _gelu_tanh(x):
    return 0.5 * x * (1.0 + jnp.tanh(0.7978845608028654 * (x + 0.044715 * (x * x * x))))


def _bdot(a, b):
    return jnp.dot(a.astype(BF16), b.astype(BF16), preferred_element_type=F32)


def _bdot_nt(a, b):
    return lax.dot_general(a.astype(BF16), b.astype(BF16), (((1,), (1,)), ((), ())),
                           preferred_element_type=F32)


def _bdot_tn(a, b):
    return lax.dot_general(a.astype(BF16), b.astype(BF16), (((0,), (0,)), ((), ())),
                           preferred_element_type=F32)


def _fdot(a, b):
    return jnp.dot(a, b, preferred_element_type=F32, precision=lax.Precision.HIGHEST)


def _ada_kernel(ct_ref, w_ref, b_ref, o_ref):
    ct = _silu(ct_ref[...])
    w = w_ref[...]
    for b in range(ct.shape[1]):
        o_ref[b:b + 1, :] = jnp.sum(ct[:, b:b + 1] * w, axis=0, keepdims=True) + b_ref[...]


def _ada(c, w_ada, b_ada):
    bsz, d = c.shape
    n = w_ada.shape[1]
    tn = 512
    return pl.pallas_call(
        _ada_kernel,
        grid=(n // tn,),
        in_specs=[pl.BlockSpec((d, bsz), lambda j: (0, 0)),
                  pl.BlockSpec((d, tn), lambda j: (0, j)),
                  pl.BlockSpec((1, tn), lambda j: (0, j))],
        out_specs=pl.BlockSpec((bsz, tn), lambda j: (0, j)),
        out_shape=jax.ShapeDtypeStruct((bsz, n), F32),
        compiler_params=_cparams(("arbitrary",)),
        name="ada",
    )(c.T, w_ada, b_ada.reshape(1, n))


def _inproj_kernel(x_ref, mod_ref, nw_ref, wm_ref, wba_ref,
                   qkv_ref, z_ref, xb_ref, yb_ref, gg_ref, gl_ref, ba_ref, bat_ref):
    x = x_ref[0]
    shift = mod_ref[0, 0:1, :]
    scale = mod_ref[0, 1:2, :]
    y = x * lax.rsqrt(jnp.mean(x * x, axis=-1, keepdims=True) + EPS)
    h = ((y * nw_ref[...]) * (1.0 + scale) + shift).astype(BF16)
    col = 0
    for ref in (qkv_ref, z_ref, xb_ref, yb_ref, gg_ref, gl_ref):
        n = ref.shape[-1]
        ref[0] = jnp.dot(h, wm_ref[:, col:col + n], preferred_element_type=F32).astype(ref.dtype)
        col += n
    wba = wba_ref[...]
    nt = (((1,), (1,)), ((), ()))
    bat_ref[0] = lax.dot_general(wba[0:8], h, nt, preferred_element_type=F32)
    ba_ref[0] = lax.dot_general(h, wba, nt, preferred_element_type=F32)[:, 0:8]


def _inproj(x, mod, norm_w, w_main, w_ba_t, ts=256):
    bsz, s, d = x.shape
    widths = (3 * GDN_WIDTH, GDN_WIDTH, LRU_WIDTH, LRU_WIDTH, d, d)
    tok = lambda n: pl.BlockSpec((1, ts, n), lambda b, i: (b, i, 0))
    out_shapes = [jax.ShapeDtypeStruct((bsz, s, n), BF16) for n in widths]
    out_shapes += [jax.ShapeDtypeStruct((bsz, s, 8), F32), jax.ShapeDtypeStruct((bsz, 8, s), F32)]
    out_specs = [tok(n) for n in widths]
    out_specs += [tok(8), pl.BlockSpec((1, 8, ts), lambda b, i: (b, 0, i))]
    return pl.pallas_call(
        _inproj_kernel,
        grid=(bsz, s // ts),
        in_specs=[tok(d),
                  pl.BlockSpec((1, 6, d), lambda b, i: (b, 0, 0)),
                  pl.BlockSpec((1, d), lambda b, i: (0, 0)),
                  pl.BlockSpec(w_main.shape, lambda b, i: (0, 0)),
                  pl.BlockSpec(w_ba_t.shape, lambda b, i: (0, 0))],
        out_specs=out_specs,
        out_shape=out_shapes,
        compiler_params=_cparams(("arbitrary", "arbitrary")),
        name="inproj",
    )(x, mod, norm_w, w_main, w_ba_t)


def _causal_conv(x_tile, w_ref, xbuf, cols=slice(None)):
    ts = x_tile.shape[0]
    xbuf[HALO:HALO + ts, cols] = x_tile
    acc = None
    for j in range(CONV_WIDTH):
        off = HALO - (CONV_WIDTH - 1) + j
        term = w_ref[j:j + 1, cols] * xbuf[off:off + ts, cols]
        acc = term if acc is None else acc + term
    xbuf[0:HALO, cols] = xbuf[ts:ts + HALO, cols]
    return acc


def _gdn_body(qkv_ref, z_ref, ba_ref, bat_ref, cw_ref, hrow_ref, hcol_ref, nw_ref, lvl_ref,
              o_ref, xbuf, s_ref, fillers=()):
    ts = qkv_ref.shape[1]
    c = GDN_CHUNK
    dk = GDN_HEAD_DIM
    qkv = _silu(_causal_conv(qkv_ref[0].astype(F32), cw_ref, xbuf))

    ba = ba_ref[0]
    g_col = -jnp.exp(hrow_ref[0:1, :]) * _softplus(ba + hrow_ref[1:2, :])
    beta_col = _sigmoid(ba)
    r_i = lax.broadcasted_iota(jnp.int32, (ts, ts), 0)
    c_i = lax.broadcasted_iota(jnp.int32, (ts, ts), 1)
    same = (r_i // c) == (c_i // c)
    incl = same & (c_i <= r_i)
    strict = same & (c_i < r_i)
    gc_col = _fdot(jnp.where(incl, 1.0, 0.0).astype(F32), g_col)
    g_row = -jnp.exp(hcol_ref[:, 0:1]) * _softplus(bat_ref[0] + hcol_ref[:, 1:2])
    gc_row = _fdot(g_row, jnp.where(same & (r_i <= c_i), 1.0, 0.0).astype(F32))
    eye = jnp.where(r_i == c_i, 1.0, 0.0).astype(F32)
    scale = dk ** -0.5
    nw = nw_ref[...]
    n_lvl = lvl_ref.shape[0]

    heads = range(GDN_HEADS)
    ks, gccs, a_bf, t_mat, qks, rhs, q_decs = [], [], [], [], [], [], []
    for h in heads:
        q = qkv[:, h * dk:(h + 1) * dk]
        k = qkv[:, GDN_WIDTH + h * dk:GDN_WIDTH + (h + 1) * dk]
        v = qkv[:, 2 * GDN_WIDTH + h * dk:2 * GDN_WIDTH + (h + 1) * dk]
        q = q * (lax.rsqrt(jnp.sum(q * q, axis=-1, keepdims=True) + EPS) * scale)
        k = k * lax.rsqrt(jnp.sum(k * k, axis=-1, keepdims=True) + EPS)
        beta = beta_col[:, h:h + 1]
        gcc = gc_col[:, GDN_HEADS + h:GDN_HEADS + h + 1]
        gcr = gc_row[GDN_HEADS + h:GDN_HEADS + h + 1, :]
        decay = jnp.exp(jnp.where(incl, gcc - gcr, 0.0))
        kb = k * beta
        m1 = _bdot_nt(jnp.concatenate([kb, q], axis=0), k)
        a_mat = jnp.where(strict, m1[:ts] * decay, 0.0)
        e_gc = jnp.exp(gcc)
        ks.append(k)
        gccs.append(gcc)
        qks.append(jnp.where(incl, m1[ts:] * decay, 0.0))
        a_bf.append(a_mat.astype(BF16))
        t_mat.append(eye - a_mat * lvl_ref[0].astype(F32))
        rhs.append(jnp.concatenate([v * beta, kb * e_gc], axis=1))
        q_decs.append(q * e_gc)
    for lv in range(1, n_lvl):
        t_bf = [t_mat[h].astype(BF16) for h in heads]
        inner = [_bdot(a_bf[h] * lvl_ref[lv], t_bf[h]) for h in heads]
        t_mat = [t_mat[h] - _bdot(t_bf[h], inner[h]) for h in heads]
    uws = [_bdot(t_mat[h], rhs[h]) for h in heads]
    states = [s_ref[h] for h in heads]
    for n in range(ts // c):
        rows = slice(n * c, (n + 1) * c)
        if n < len(fillers):
            fillers[n]()
        for h in heads:
            gc_last = gccs[h][(n + 1) * c - 1:(n + 1) * c, :]
            k_dec = ks[h][rows] * jnp.exp(gc_last - gccs[h][rows])
            ws = _bdot(jnp.concatenate([uws[h][rows, dk:], q_decs[h][rows]], axis=0), states[h])
            v_new = uws[h][rows, :dk] - ws[:c]
            o = ws[c:] + _bdot(qks[h][rows, n * c:(n + 1) * c], v_new)
            states[h] = states[h] * jnp.exp(gc_last) + _bdot_tn(k_dec, v_new)
            zg = _silu(z_ref[0, rows, h * dk:(h + 1) * dk].astype(F32))
            o = o * lax.rsqrt(jnp.mean(o * o, axis=-1, keepdims=True) + EPS) * nw * zg
            o_ref[0, rows, h * dk:(h + 1) * dk] = o.astype(o_ref.dtype)
    for h in heads:
        s_ref[h] = states[h]


def _gdn_level_masks(ts):
    i = jnp.arange(ts)[:, None]
    j = jnp.arange(ts)[None, :]
    masks = []
    s = 1
    while s < GDN_CHUNK:
        masks.append(((i // (2 * s)) == (j // (2 * s))) & ((i // s) != (j // s)))
        s *= 2
    return jnp.stack(masks).astype(BF16)


def _lru_body(xb_ref, yb_ref, cw_ref, cb_ref, wa_ref, ba_ref, wx_ref, bx_ref, lam_ref,
              o_ref, xbuf, carry_ref, cols):
    ts = xb_ref.shape[1]
    xc = _causal_conv(xb_ref[0, :, cols].astype(F32), cw_ref, xbuf, cols) + cb_ref[:, cols]
    r = _sigmoid(_bdot(xc, wa_ref[cols, cols]) + ba_ref[:, cols])
    i = _sigmoid(_bdot(xc, wx_ref[cols, cols]) + bx_ref[:, cols])
    log_a = (-LRU_C) * r * _softplus(-lam_ref[:, cols])
    a = jnp.exp(log_a)
    mult = jnp.sqrt(jnp.maximum(1.0 - jnp.exp(2.0 * log_a), 0.0))
    b = mult * (i * xc)
    row = lax.broadcasted_iota(jnp.int32, a.shape, 0)
    d = 1
    while d < ts:
        keep = row >= d
        a_s = jnp.where(keep, pltpu.roll(a, d, axis=0), 1.0)
        b_s = jnp.where(keep, pltpu.roll(b, d, axis=0), 0.0)
        b = a * b_s + b
        a = a * a_s
        d *= 2
    h = a * carry_ref[:, cols] + b
    carry_ref[:, cols] = h[ts - 1:ts, :]
    o_ref[0, :, cols] = (h * _gelu_tanh(yb_ref[0, :, cols].astype(F32))).astype(o_ref.dtype)


N_GDN_IN, N_LRU_IN = 9, 9


def _mixers_kernel(*refs):
    gdn_in = refs[:N_GDN_IN]
    lru_in = refs[N_GDN_IN:N_GDN_IN + N_LRU_IN]
    o_gdn_ref, o_lru_ref, gdn_xbuf, s_ref, lru_xbuf, carry_ref = refs[N_GDN_IN + N_LRU_IN:]

    @pl.when(pl.program_id(1) == 0)
    def _():
        s_ref[...] = jnp.zeros(s_ref.shape, F32)
        carry_ref[...] = jnp.zeros(carry_ref.shape, F32)
        gdn_xbuf[0:HALO, :] = jnp.zeros((HALO, gdn_xbuf.shape[1]), F32)
        lru_xbuf[0:HALO, :] = jnp.zeros((HALO, lru_xbuf.shape[1]), F32)

    groups = [slice(g * LRU_GROUP, (g + 1) * LRU_GROUP) for g in range(LRU_WIDTH // LRU_GROUP)]
    _gdn_body(*gdn_in, o_gdn_ref, gdn_xbuf, s_ref,
              fillers=[functools.partial(_lru_body, *lru_in, o_lru_ref, lru_xbuf, carry_ref, cols)
                       for cols in groups])


def _mixers(gdn_args, lru_args, ts=256):
    qkv, z, ba, bat = gdn_args[:4]
    xb, yb = lru_args[:2]
    bsz, s, _ = qkv.shape
    lvl = _gdn_level_masks(ts)
    gdn_args = tuple(gdn_args) + (lvl,)
    assert len(gdn_args) == N_GDN_IN and len(lru_args) == N_LRU_IN
    tok = lambda n: pl.BlockSpec((1, ts, n), lambda b, i: (b, i, 0))
    full = lambda a: pl.BlockSpec(a.shape, lambda b, i: (0,) * a.ndim)
    in_specs = ([tok(3 * GDN_WIDTH), tok(GDN_WIDTH), tok(8), pl.BlockSpec((1, 8, ts), lambda b, i: (b, 0, i))]
                + [full(a) for a in gdn_args[4:]]
                + [tok(LRU_WIDTH), tok(LRU_WIDTH)] + [full(a) for a in lru_args[2:]])
    return pl.pallas_call(
        _mixers_kernel,
        grid=(bsz, s // ts),
        in_specs=in_specs,
        out_specs=[tok(GDN_WIDTH), tok(LRU_WIDTH)],
        out_shape=[jax.ShapeDtypeStruct((bsz, s, GDN_WIDTH), BF16),
                   jax.ShapeDtypeStruct((bsz, s, LRU_WIDTH), BF16)],
        scratch_shapes=[pltpu.VMEM((ts + HALO, 3 * GDN_WIDTH), F32),
                        pltpu.VMEM((GDN_HEADS, GDN_HEAD_DIM, GDN_HEAD_DIM), F32),
                        pltpu.VMEM((ts + HALO, LRU_WIDTH), F32), pltpu.VMEM((1, LRU_WIDTH), F32)],
        compiler_params=_cparams(("arbitrary", "arbitrary")),
        name="mixers",
    )(*gdn_args, *lru_args)


def _merge_kernel(x_ref, og_ref, ol_ref, gg_ref, gl_ref, mod_ref, nw_ref,
                  wg_ref, wl_ref, wo_ref, wrt_ref, br_ref,
                  x1_ref, h2_ref, idx_ref, gate_ref, rank_ref, cnt_ref, carry_ref):
    tm = x_ref.shape[1]
    step = pl.program_id(0) * pl.num_programs(1) + pl.program_id(1)

    @pl.when(step == 0)
    def _():
        carry_ref[...] = jnp.zeros(carry_ref.shape, F32)

    gate_m = mod_ref[0, 2:3, :]
    shift_f = mod_ref[0, 3:4, :]
    scale_f = mod_ref[0, 4:5, :]
    merged = (_sigmoid(gg_ref[0].astype(F32)) * jnp.dot(og_ref[0], wg_ref[...], preferred_element_type=F32)
              + _sigmoid(gl_ref[0].astype(F32)) * jnp.dot(ol_ref[0], wl_ref[...], preferred_element_type=F32))
    x1 = x_ref[0] + gate_m * _bdot(merged, wo_ref[...])
    x1_ref[0] = x1
    y = x1 * lax.rsqrt(jnp.mean(x1 * x1, axis=-1, keepdims=True) + EPS)
    h2 = (y * nw_ref[...]) * (1.0 + scale_f) + shift_f
    h2_ref[...] = h2.reshape(tm, 1, h2.shape[-1])

    logits = lax.dot_general(wrt_ref[...], h2, (((1,), (1,)), ((), ())),
                             preferred_element_type=F32, precision=lax.Precision.HIGHEST) + br_ref[...]
    e_iota = lax.broadcasted_iota(jnp.int32, logits.shape, 0)
    work = logits
    sel = jnp.zeros(logits.shape, F32)
    vals, idxs = [], []
    for _ in range(TOP_K):
        m = jnp.max(work, axis=0, keepdims=True)
        idx = jnp.min(jnp.where(work == m, e_iota, N_EXPERTS), axis=0, keepdims=True)
        hit = e_iota == idx
        sel = jnp.where(hit, 1.0, sel)
        work = jnp.where(hit, -jnp.inf, work)
        vals.append(m)
        idxs.append(idx)
    exps = [jnp.exp(v - vals[0]) for v in vals]
    denom = exps[0] + exps[1] + exps[2] + exps[3]
    r_i = lax.broadcasted_iota(jnp.int32, (tm, tm), 0)
    c_i = lax.broadcasted_iota(jnp.int32, (tm, tm), 1)
    before = jnp.where(r_i < c_i, 1.0, 0.0).astype(BF16)
    prefix = jnp.dot(sel.astype(BF16), before, preferred_element_type=F32) + carry_ref[...]
    for kk in range(TOP_K):
        idx_ref[0, kk:kk + 1, :] = idxs[kk]
        gate_ref[0, kk:kk + 1, :] = exps[kk] / denom
        rank = jnp.sum(jnp.where(e_iota == idxs[kk], prefix, 0.0), axis=0, keepdims=True)
        rank_ref[0, kk:kk + 1, :] = rank.astype(jnp.int32)
    total = carry_ref[...] + jnp.sum(sel, axis=1, keepdims=True)
    carry_ref[...] = total
    cnt_ref[...] = total.astype(jnp.int32)


def _merge(x, og, ol, gg, gl, mod, norm_w, wg, wl, wo, wrt, br, tm=256):
    bsz, s, d = x.shape
    ns = s // tm
    tok = lambda n: pl.BlockSpec((1, tm, n), lambda b, i: (b, i, 0))
    full = lambda a: pl.BlockSpec(a.shape, lambda b, i: (0,) * a.ndim)
    lane = pl.BlockSpec((1, TOP_K, tm), lambda b, i: (b * ns + i, 0, 0))
    nt = bsz * ns
    return pl.pallas_call(
        _merge_kernel,
        grid=(bsz, ns),
        in_specs=[tok(d), tok(GDN_WIDTH), tok(LRU_WIDTH), tok(d), tok(d),
                  pl.BlockSpec((1, 6, d), lambda b, i: (b, 0, 0)),
                  full(norm_w), full(wg), full(wl), full(wo), full(wrt), full(br)],
        out_specs=[tok(d),
                   pl.BlockSpec((tm, 1, d), lambda b, i: (b * ns + i, 0, 0)),
                   lane, lane, lane,
                   pl.BlockSpec((N_EXPERTS, 1), lambda b, i: (0, 0))],
        out_shape=[jax.ShapeDtypeStruct((bsz, s, d), F32),
                   jax.ShapeDtypeStruct((bsz * s, 1, d), F32),
                   jax.ShapeDtypeStruct((nt, TOP_K, tm), jnp.int32),
                   jax.ShapeDtypeStruct((nt, TOP_K, tm), F32),
                   jax.ShapeDtypeStruct((nt, TOP_K, tm), jnp.int32),
                   jax.ShapeDtypeStruct((N_EXPERTS, 1), jnp.int32)],
        scratch_shapes=[pltpu.VMEM((N_EXPERTS, 1), F32)],
        compiler_params=_cparams(("arbitrary", "arbitrary")),
        name="merge",
    )(x, og, ol, gg, gl, mod, norm_w, wg, wl, wo, wrt, br)


def _dispatch_kernel(dest_ref, zinfo_ref, h_ref, xs_ref, zbuf, sem, zsem):
    tm = h_ref.shape[0]
    n_blocks = xs_ref.shape[0] // MOE_BLOCK

    def row_copy(r, slot):
        return pltpu.make_async_copy(h_ref.at[r], xs_ref.at[slot], sem)

    def issue(r, carry):
        for kk in range(TOP_K):
            row_copy(r, dest_ref[0, 0, kk * tm + r]).start()
        return carry

    lax.fori_loop(0, tm, issue, 0)

    @pl.when(pl.program_id(0) == 0)
    def _():
        zbuf[...] = jnp.zeros(zbuf.shape, F32)

        def zero_copy(start, rows):
            return pltpu.make_async_copy(zbuf.at[pl.ds(0, rows)], xs_ref.at[pl.ds(start, rows)], zsem)

        def pad_runs(wait):
            def per_expert(e, carry):
                start = zinfo_ref[0, e]
                length = zinfo_ref[0, N_EXPERTS + e]
                rows = MOE_BLOCK // 2
                while rows >= 1:
                    @pl.when((length & rows) != 0)
                    def _(rows=rows):
                        cp = zero_copy(start + (length & (-2 * rows)), rows)
                        cp.wait() if wait else cp.start()
                    rows //= 2
                return carry

            lax.fori_loop(0, N_EXPERTS, per_expert, 0)

            def per_block(b, carry):
                cp = zero_copy(b * MOE_BLOCK, MOE_BLOCK)
                cp.wait() if wait else cp.start()
                return carry

            lax.fori_loop(zinfo_ref[0, 2 * N_EXPERTS], n_blocks, per_block, 0)

        pad_runs(False)
        pad_runs(True)

    def wait(r, carry):
        for kk in range(TOP_K):
            row_copy(r, 0).wait()
        return carry

    lax.fori_loop(0, tm, wait, 0)


def _dispatch(h2_rows, dest, zinfo, n_slots, tm=256):
    n_tok, _, d = h2_rows.shape
    nt = n_tok // tm
    return pl.pallas_call(
        _dispatch_kernel,
        grid=(nt,),
        in_specs=[pl.BlockSpec((1, 1, TOP_K * tm), lambda i: (i, 0, 0), memory_space=pltpu.SMEM),
                  pl.BlockSpec(zinfo.shape, lambda i: (0, 0), memory_space=pltpu.SMEM),
                  pl.BlockSpec((tm, 1, d), lambda i: (i, 0, 0))],
        out_specs=pl.BlockSpec(memory_space=pl.ANY),
        out_shape=jax.ShapeDtypeStruct((n_slots, 1, d), F32),
        scratch_shapes=[pltpu.VMEM((MOE_BLOCK, 1, d), F32), pltpu.SemaphoreType.DMA,
                        pltpu.SemaphoreType.DMA],
        compiler_params=_cparams(("arbitrary",)),
        name="dispatch",
    )(dest, zinfo, h2_rows)


def _expert_kernel(be_ref, nu_ref, win_ref, nval_ref, spare_ref, src_ref,
                   xs_ref, w1_ref, b1g_ref, b1l_ref, w2_ref, b2_ref, perm_ref,
                   yt_ref, x2d, w1p, w2b, yb0, yb1, yb2, sems):
    blk, _, d = xs_ref.shape
    f = w2b.shape[0]
    n_groups = 2 * f // EXPERT_COLS
    half = EXPERT_COLS // 2
    b = pl.program_id(0)
    n_blocks = pl.num_programs(0) - 1
    bufs = (yb0, yb1, yb2)
    live = b < nu_ref[0]
    b_w = jnp.minimum(b, n_blocks - 1)
    fresh = (b == 0) | (be_ref[b_w] != be_ref[jnp.maximum(b_w - 1, 0)])

    b_prev = jnp.maximum(b - 1, 0)
    win, n_val, spare = win_ref[b_prev], nval_ref[b_prev], spare_ref[b_prev]

    def row_copy(buf, sem, j):
        dst = jnp.where(j < n_val, src_ref[win + j], spare + j)
        return pltpu.make_async_copy(buf.at[j], yt_ref.at[dst], sem)

    def tile_wait(buf, sem):
        pltpu.make_async_copy(buf, yt_ref.at[pl.ds(0, blk)], sem).wait()

    def ffn(out_buf):
        x2d[...] = xs_ref[...].reshape(blk, d)
        u = jnp.dot(x2d[...].astype(BF16), w1p[...], preferred_element_type=F32)
        acts = []
        for j in range(n_groups):
            u_glu = u[:, j * EXPERT_COLS:j * EXPERT_COLS + half] + b1g_ref[0, :, j * half:(j + 1) * half]
            u_lin = u[:, j * EXPERT_COLS + half:(j + 1) * EXPERT_COLS] + b1l_ref[0, :, j * half:(j + 1) * half]
            u_glu = jnp.minimum(u_glu, SWIGLU_LIMIT)
            u_lin = jnp.clip(u_lin, -SWIGLU_LIMIT, SWIGLU_LIMIT)
            acts.append((u_glu * _sigmoid(SWIGLU_ALPHA * u_glu) * (u_lin + 1.0)).astype(BF16))
        act = jnp.concatenate(acts, axis=1)
        y = jnp.dot(act, w2b[...], preferred_element_type=F32) + b2_ref[0]
        out_buf[...] = y.reshape(blk, 1, d)

    @pl.when(live & fresh)
    def _():
        for j in range(n_groups):
            cols = slice(j * EXPERT_COLS, (j + 1) * EXPERT_COLS)
            w1p[:, cols] = jnp.dot(w1_ref[0, :, cols].astype(BF16), perm_ref[...],
                                   preferred_element_type=F32).astype(BF16)
        w2b[...] = w2_ref[0].astype(BF16)

    for p in range(3):
        cur, prv = bufs[p], bufs[(p + 2) % 3]
        sem_cur, sem_prv, sem_old = sems.at[p], sems.at[(p + 2) % 3], sems.at[(p + 1) % 3]
        mine = b % 3 == p

        @pl.when(mine & (b >= 3))
        def _(cur=cur, sem_cur=sem_cur):
            tile_wait(cur, sem_cur)

        @pl.when(mine & (b >= 1) & live)
        def _(cur=cur, prv=prv, sem_prv=sem_prv):
            for j in range(blk):
                row_copy(prv, sem_prv, j).start()
            ffn(cur)

        @pl.when(mine & ((b == 0) | jnp.logical_not(live)))
        def _(cur=cur, prv=prv, old=bufs[(p + 1) % 3], sem_prv=sem_prv, sem_old=sem_old):
            @pl.when(b >= 1)
            def _():
                def issue(j, carry):
                    row_copy(prv, sem_prv, j).start()
                    return carry

                lax.fori_loop(0, blk, issue, 0)

            @pl.when(live)
            def _():
                ffn(cur)

            @pl.when(jnp.logical_not(live) & (b < n_blocks))
            def _():
                cur[...] = jnp.zeros(cur.shape, F32)

            @pl.when(b == n_blocks)
            def _():
                tile_wait(old, sem_old)
                tile_wait(prv, sem_prv)


def _pair_split_matrix():
    i = jnp.arange(EXPERT_COLS)[:, None]
    j = jnp.arange(EXPERT_COLS)[None, :]
    half = EXPERT_COLS // 2
    return jnp.where(j < half, i == 2 * j, i == 2 * (j - half) + 1).astype(BF16)


def _experts(block_e, n_used, win_start, n_valid, spare0, src_sorted, xs, w1, b1g, b1l, w2, b2, n_rows_out):
    n_slots, _, d = xs.shape
    f = w2.shape[1]
    n_blocks = n_slots // MOE_BLOCK
    perm = _pair_split_matrix()
    slot = lambda b, be, nu, *_: (jnp.minimum(b, nu[0] - 1), 0, 0)
    wsel = lambda b, be, nu, *_: (be[jnp.minimum(b, n_blocks - 1)], 0, 0)
    row_buf = pltpu.VMEM((MOE_BLOCK, 1, d), F32)
    grid_spec = pltpu.PrefetchScalarGridSpec(
        num_scalar_prefetch=6,
        grid=(n_blocks + 1,),
        in_specs=[pl.BlockSpec((MOE_BLOCK, 1, d), slot),
                  pl.BlockSpec((1, d, 2 * f), wsel),
                  pl.BlockSpec((1, 1, f), wsel), pl.BlockSpec((1, 1, f), wsel),
                  pl.BlockSpec((1, f, d), wsel), pl.BlockSpec((1, 1, d), wsel),
                  pl.BlockSpec(perm.shape, lambda b, *_: (0, 0))],
        out_specs=pl.BlockSpec(memory_space=pl.ANY),
        scratch_shapes=[pltpu.VMEM((MOE_BLOCK, d), F32), pltpu.VMEM((d, 2 * f), BF16),
                        pltpu.VMEM((f, d), BF16), row_buf, row_buf, row_buf,
                        pltpu.SemaphoreType.DMA((3,))],
    )
    return pl.pallas_call(
        _expert_kernel,
        grid_spec=grid_spec,
        out_shape=jax.ShapeDtypeStruct((n_rows_out, 1, d), F32),
        compiler_params=pltpu.CompilerParams(dimension_semantics=("arbitrary",),
                                             vmem_limit_bytes=EXPERT_VMEM_LIMIT),
        name="experts",
    )(block_e, n_used, win_start, n_valid, spare0, src_sorted, xs, w1, b1g, b1l, w2, b2, perm)


def _combine_kernel(x1_ref, gate_ref, mod_ref, nw_ref, yt_ref, o_ref, buf2):
    tm = x1_ref.shape[1]
    d = x1_ref.shape[2]
    buf2[...] = yt_ref[...].reshape(TOP_K * tm, d)
    gate = gate_ref[0]
    acc = None
    for kk in range(TOP_K):
        term = gate[:, kk:kk + 1] * buf2[kk * tm:(kk + 1) * tm, :]
        acc = term if acc is None else acc + term
    x2 = x1_ref[0] + mod_ref[0, 5:6, :] * acc
    o_ref[0] = x2 * lax.rsqrt(jnp.mean(x2 * x2, axis=-1, keepdims=True) + EPS) * nw_ref[...]


def _combine(x1, gate, mod, norm_w, yt, tm=256):
    bsz, s, d = x1.shape
    ns = s // tm
    tok = lambda n: pl.BlockSpec((1, tm, n), lambda b, i: (b, i, 0))
    return pl.pallas_call(
        _combine_kernel,
        grid=(bsz, ns),
        in_specs=[tok(d),
                  pl.BlockSpec((1, tm, TOP_K), lambda b, i: (b * ns + i, 0, 0)),
                  pl.BlockSpec((1, 6, d), lambda b, i: (b, 0, 0)),
                  pl.BlockSpec((1, d), lambda b, i: (0, 0)),
                  pl.BlockSpec((TOP_K * tm, 1, d), lambda b, i: (b * ns + i, 0, 0))],
        out_specs=tok(d),
        out_shape=jax.ShapeDtypeStruct((bsz, s, d), F32),
        scratch_shapes=[pltpu.VMEM((TOP_K * tm, d), F32)],
        compiler_params=_cparams(("arbitrary", "arbitrary")),
        name="combine",
    )(x1, gate, mod, norm_w, yt)


def _block_diag(w):
    nb, bi, bo = w.shape
    eye = jnp.eye(nb, dtype=w.dtype)
    return (eye[:, None, :, None] * w[:, :, None, :]).reshape(nb * bi, nb * bo)


def _layer(x, c, w_ada, b_ada, norm_mix, norm_ffn, w_in, gdn_conv_w, gdn_a_log, gdn_dt_bias,
           gdn_norm_w, lru_conv_w, lru_conv_b, lru_w_a, lru_b_a, lru_w_x, lru_b_x, lru_lambda,
           w_branch_gdn, w_branch_lru, w_out, w_router, b_router, w1, b1, w2, b2, norm_final):
    bsz, s, d = x.shape
    n_tok = bsz * s
    tm = 256
    row = lambda v: v.reshape(1, -1).astype(F32)

    mod = _ada(c, w_ada, b_ada).reshape(bsz, 6, d)

    o_ba = 4 * GDN_WIDTH
    w_main = jnp.concatenate([w_in[:, :o_ba], w_in[:, o_ba + 2 * GDN_HEADS:]], axis=1).astype(BF16)
    w_ba_t = jnp.pad(w_in[:, o_ba:o_ba + 2 * GDN_HEADS].T, ((0, 128 - 2 * GDN_HEADS), (0, 0))).astype(BF16)
    qkv, z, xb, yb, gg, gl, ba, bat = _inproj(x, mod, row(norm_mix), w_main, w_ba_t)

    zeros4 = jnp.zeros((GDN_HEADS,), F32)
    hvals = jnp.stack([jnp.concatenate([zeros4, gdn_a_log.astype(F32)]),
                       jnp.concatenate([zeros4, gdn_dt_bias.astype(F32)])])
    o_gdn, o_lru = _mixers(
        (qkv, z, ba, bat, gdn_conv_w.astype(F32), hvals, hvals.T, row(gdn_norm_w)),
        (xb, yb, lru_conv_w.astype(F32), row(lru_conv_b),
         _block_diag(lru_w_a).astype(BF16), row(lru_b_a),
         _block_diag(lru_w_x).astype(BF16), row(lru_b_x), row(lru_lambda)))

    x1, h2_rows, idx_t, gate_t, rank_t, counts = _merge(
        x, o_gdn, o_lru, gg, gl, mod, row(norm_ffn),
        w_branch_gdn.astype(BF16), w_branch_lru.astype(BF16), w_out.astype(BF16),
        w_router.T.astype(F32), b_router.reshape(-1, 1).astype(F32), tm=tm)

    counts = counts.reshape(N_EXPERTS)
    padded = (counts + MOE_BLOCK - 1) // MOE_BLOCK * MOE_BLOCK
    pend = jnp.cumsum(padded)
    pstart = pend - padded
    assert (n_tok * TOP_K) % MOE_BLOCK == 0 and s % tm == 0
    n_blocks = (n_tok * TOP_K) // MOE_BLOCK + N_EXPERTS
    n_slots = n_blocks * MOE_BLOCK
    block_first = jnp.arange(n_blocks, dtype=jnp.int32) * MOE_BLOCK
    block_e = jnp.minimum(jnp.sum(pend[None, :] <= block_first[:, None], axis=1),
                          N_EXPERTS - 1).astype(jnp.int32)
    n_used = (pend[-1:] // MOE_BLOCK).astype(jnp.int32)
    onehot = idx_t[..., None] == jnp.arange(N_EXPERTS, dtype=jnp.int32)
    dest = (rank_t + jnp.sum(jnp.where(onehot, pstart.astype(jnp.int32), 0), axis=-1)).astype(jnp.int32)
    dest = dest.reshape(n_tok // tm, 1, TOP_K * tm)
    zinfo = jnp.concatenate([pstart + counts, padded - counts, n_used,
                             jnp.zeros((128 - 2 * N_EXPERTS - 1,), jnp.int32)]).astype(jnp.int32)

    n_assign = n_tok * TOP_K
    _, src_sorted = lax.sort_key_val(dest.reshape(-1), jnp.arange(n_assign, dtype=jnp.int32))
    src_sorted = jnp.concatenate([src_sorted, jnp.zeros((MOE_BLOCK,), jnp.int32)])
    cstart = (jnp.cumsum(counts) - counts).astype(jnp.int32)
    seg_rank = block_first - pstart[block_e].astype(jnp.int32)
    n_valid = jnp.clip(counts[block_e].astype(jnp.int32) - seg_rank, 0, MOE_BLOCK).astype(jnp.int32)
    win_start = jnp.clip(cstart[block_e] + seg_rank, 0, n_assign).astype(jnp.int32)
    spare0 = (n_assign + block_first - cstart[block_e] - counts[block_e].astype(jnp.int32)).astype(jnp.int32)

    xs = _dispatch(h2_rows, dest, zinfo.reshape(1, 128), n_slots, tm=tm)
    yt = _experts(block_e, n_used, win_start, n_valid, spare0, src_sorted, xs, w1.astype(F32),
                  b1[:, None, 0::2].astype(F32), b1[:, None, 1::2].astype(F32),
                  w2.astype(F32), b2[:, None, :].astype(F32), n_assign + N_EXPERTS * MOE_BLOCK)
    gate = jnp.swapaxes(gate_t, 1, 2)
    return _combine(x1, gate, mod, row(norm_final), yt, tm=tm)


def kernel(x, c, w_ada, b_ada, norm_mix, norm_ffn, w_in, gdn_conv_w, gdn_a_log, gdn_dt_bias,
           gdn_norm_w, lru_conv_w, lru_conv_b, lru_w_a, lru_b_a, lru_w_x, lru_b_x, lru_lambda,
           w_branch_gdn, w_branch_lru, w_out, w_router, b_router, w1, b1, w2, b2, norm_final):
    assert w_ada.shape[0] == 1, "single-layer block"
    return _layer(x, c, w_ada[0], b_ada[0], norm_mix[0], norm_ffn[0], w_in[0], gdn_conv_w[0],
                  gdn_a_log[0], gdn_dt_bias[0], gdn_norm_w[0], lru_conv_w[0], lru_conv_b[0],
                  lru_w_a[0], lru_b_a[0], lru_w_x[0], lru_b_x[0], lru_lambda[0],
                  w_branch_gdn[0], w_branch_lru[0], w_out[0], w_router[0], b_router[0],
                  w1[0], b1[0], w2[0], b2[0], norm_final)
```

```python
import functools

import jax
import jax.numpy as jnp
from jax import lax
from jax.experimental import pallas as pl
from jax.experimental.pallas import tpu as pltpu

F32 = jnp.float32
BF16 = jnp.bfloat16

EPS = 1e-6
GDN_HEADS = 4
GDN_HEAD_DIM = 128
GDN_WIDTH = GDN_HEADS * GDN_HEAD_DIM
GDN_CHUNK = 64
LRU_WIDTH = 512
LRU_C = 8.0
LRU_GROUP = 128
CONV_WIDTH = 4
N_EXPERTS = 32
TOP_K = 4
SWIGLU_ALPHA = 1.702
SWIGLU_LIMIT = 7.0
MOE_BLOCK = 512
EXPERT_COLS = 256

HALO = 8
VMEM_LIMIT = 56 * 1024 * 1024
EXPERT_VMEM_LIMIT = 60 * 1024 * 1024


def _cparams(sem):
    return pltpu.CompilerParams(dimension_semantics=sem, vmem_limit_bytes=VMEM_LIMIT)


def _sigmoid(x):
    return 1.0 / (1.0 + jnp.exp(-x))


def _silu(x):
    return x * _sigmoid(x)


def _softplus(x):
    return jnp.maximum(x, 0.0) + jnp.log1p(jnp.exp(-jnp.abs(x)))


def _gelu_tanh(x):
    return 0.5 * x * (1.0 + jnp.tanh(0.7978845608028654 * (x + 0.044715 * (x * x * x))))


def _bdot(a, b):
    return jnp.dot(a.astype(BF16), b.astype(BF16), preferred_element_type=F32)


def _bdot_nt(a, b):
    return lax.dot_general(a.astype(BF16), b.astype(BF16), (((1,), (1,)), ((), ())),
                           preferred_element_type=F32)


def _bdot_tn(a, b):
    return lax.dot_general(a.astype(BF16), b.astype(BF16), (((0,), (0,)), ((), ())),
                           preferred_element_type=F32)


def _fdot(a, b):
    return jnp.dot(a, b, preferred_element_type=F32, precision=lax.Precision.HIGHEST)


def _ada_kernel(ct_ref, w_ref, b_ref, o_ref):
    ct = _silu(ct_ref[...])
    w = w_ref[...]
    for b in range(ct.shape[1]):
        o_ref[b:b + 1, :] = jnp.sum(ct[:, b:b + 1] * w, axis=0, keepdims=True) + b_ref[...]


def _ada(c, w_ada, b_ada):
    bsz, d = c.shape
    n = w_ada.shape[1]
    tn = 512
    return pl.pallas_call(
        _ada_kernel,
        grid=(n // tn,),
        in_specs=[pl.BlockSpec((d, bsz), lambda j: (0, 0)),
                  pl.BlockSpec((d, tn), lambda j: (0, j)),
                  pl.BlockSpec((1, tn), lambda j: (0, j))],
        out_specs=pl.BlockSpec((bsz, tn), lambda j: (0, j)),
        out_shape=jax.ShapeDtypeStruct((bsz, n), F32),
        compiler_params=_cparams(("arbitrary",)),
        name="ada",
    )(c.T, w_ada, b_ada.reshape(1, n))


def _inproj_kernel(x_ref, mod_ref, nw_ref, wm_ref, wba_ref,
                   qkv_ref, z_ref, xb_ref, yb_ref, gg_ref, gl_ref, ba_ref, bat_ref):
    x = x_ref[0]
    shift = mod_ref[0, 0:1, :]
    scale = mod_ref[0, 1:2, :]
    y = x * lax.rsqrt(jnp.mean(x * x, axis=-1, keepdims=True) + EPS)
    h = ((y * nw_ref[...]) * (1.0 + scale) + shift).astype(BF16)
    col = 0
    for ref in (qkv_ref, z_ref, xb_ref, yb_ref, gg_ref, gl_ref):
        n = ref.shape[-1]
        ref[0] = jnp.dot(h, wm_ref[:, col:col + n], preferred_element_type=F32).astype(ref.dtype)
        col += n
    wba = wba_ref[...]
    nt = (((1,), (1,)), ((), ()))
    bat_ref[0] = lax.dot_general(wba[0:8], h, nt, preferred_element_type=F32)
    ba_ref[0] = lax.dot_general(h, wba, nt, preferred_element_type=F32)[:, 0:8]


def _inproj(x, mod, norm_w, w_main, w_ba_t, ts=256):
    bsz, s, d = x.shape
    widths = (3 * GDN_WIDTH, GDN_WIDTH, LRU_WIDTH, LRU_WIDTH, d, d)
    tok = lambda n: pl.BlockSpec((1, ts, n), lambda b, i: (b, i, 0))
    out_shapes = [jax.ShapeDtypeStruct((bsz, s, n), BF16) for n in widths]
    out_shapes += [jax.ShapeDtypeStruct((bsz, s, 8), F32), jax.ShapeDtypeStruct((bsz, 8, s), F32)]
    out_specs = [tok(n) for n in widths]
    out_specs += [tok(8), pl.BlockSpec((1, 8, ts), lambda b, i: (b, 0, i))]
    return pl.pallas_call(
        _inproj_kernel,
        grid=(bsz, s // ts),
        in_specs=[tok(d),
                  pl.BlockSpec((1, 6, d), lambda b, i: (b, 0, 0)),
                  pl.BlockSpec((1, d), lambda b, i: (0, 0)),
                  pl.BlockSpec(w_main.shape, lambda b, i: (0, 0)),
                  pl.BlockSpec(w_ba_t.shape, lambda b, i: (0, 0))],
        out_specs=out_specs,
        out_shape=out_shapes,
        compiler_params=_cparams(("arbitrary", "arbitrary")),
        name="inproj",
    )(x, mod, norm_w, w_main, w_ba_t)


def _causal_conv(x_tile, w_ref, xbuf, cols=slice(None)):
    ts = x_tile.shape[0]
    xbuf[HALO:HALO + ts, cols] = x_tile
    acc = None
    for j in range(CONV_WIDTH):
        off = HALO - (CONV_WIDTH - 1) + j
        term = w_ref[j:j + 1, cols] * xbuf[off:off + ts, cols]
        acc = term if acc is None else acc + term
    xbuf[0:HALO, cols] = xbuf[ts:ts + HALO, cols]
    return acc


def _gdn_body(qkv_ref, z_ref, ba_ref, bat_ref, cw_ref, hrow_ref, hcol_ref, nw_ref, lvl_ref,
              o_ref, xbuf, s_ref, fillers=()):
    ts = qkv_ref.shape[1]
    c = GDN_CHUNK
    dk = GDN_HEAD_DIM
    qkv = _silu(_causal_conv(qkv_ref[0].astype(F32), cw_ref, xbuf))

    ba = ba_ref[0]
    g_col = -jnp.exp(hrow_ref[0:1, :]) * _softplus(ba + hrow_ref[1:2, :])
    beta_col = _sigmoid(ba)
    r_i = lax.broadcasted_iota(jnp.int32, (ts, ts), 0)
    c_i = lax.broadcasted_iota(jnp.int32, (ts, ts), 1)
    same = (r_i // c) == (c_i // c)
    incl = same & (c_i <= r_i)
    strict = same & (c_i < r_i)
    gc_col = _fdot(jnp.where(incl, 1.0, 0.0).astype(F32), g_col)
    g_row = -jnp.exp(hcol_ref[:, 0:1]) * _softplus(bat_ref[0] + hcol_ref[:, 1:2])
    gc_row = _fdot(g_row, jnp.where(same & (r_i <= c_i), 1.0, 0.0).astype(F32))
    eye = jnp.where(r_i == c_i, 1.0, 0.0).astype(F32)
    scale = dk ** -0.5
    nw = nw_ref[...]
    n_lvl = lvl_ref.shape[0]

    heads = range(GDN_HEADS)
    ks, gccs, a_bf, t_mat, qks, rhs, q_decs = [], [], [], [], [], [], []
    for h in heads:
        q = qkv[:, h * dk:(h + 1) * dk]
        k = qkv[:, GDN_WIDTH + h * dk:GDN_WIDTH + (h + 1) * dk]
        v = qkv[:, 2 * GDN_WIDTH + h * dk:2 * GDN_WIDTH + (h + 1) * dk]
        q = q * (lax.rsqrt(jnp.sum(q * q, axis=-1, keepdims=True) + EPS) * scale)
        k = k * lax.rsqrt(jnp.sum(k * k, axis=-1, keepdims=True) + EPS)
        beta = beta_col[:, h:h + 1]
        gcc = gc_col[:, GDN_HEADS + h:GDN_HEADS + h + 1]
        gcr = gc_row[GDN_HEADS + h:GDN_HEADS + h + 1, :]
        decay = jnp.exp(jnp.where(incl, gcc - gcr, 0.0))
        kb = k * beta
        m1 = _bdot_nt(jnp.concatenate([kb, q], axis=0), k)
        a_mat = jnp.where(strict, m1[:ts] * decay, 0.0)
        e_gc = jnp.exp(gcc)
        ks.append(k)
        gccs.append(gcc)
        qks.append(jnp.where(incl, m1[ts:] * decay, 0.0))
        a_bf.append(a_mat.astype(BF16))
        t_mat.append(eye - a_mat * lvl_ref[0].astype(F32))
        rhs.append(jnp.concatenate([v * beta, kb * e_gc], axis=1))
        q_decs.append(q * e_gc)
    for lv in range(1, n_lvl):
        t_bf = [t_mat[h].astype(BF16) for h in heads]
        inner = [_bdot(a_bf[h] * lvl_ref[lv], t_bf[h]) for h in heads]
        t_mat = [t_mat[h] - _bdot(t_bf[h], inner[h]) for h in heads]
    uws = [_bdot(t_mat[h], rhs[h]) for h in heads]
    states = [s_ref[h] for h in heads]
    for n in range(ts // c):
        rows = slice(n * c, (n + 1) * c)
        if n < len(fillers):
            fillers[n]()
        for h in heads:
            gc_last = gccs[h][(n + 1) * c - 1:(n + 1) * c, :]
            k_dec = ks[h][rows] * jnp.exp(gc_last - gccs[h][rows])
            ws = _bdot(jnp.concatenate([uws[h][rows, dk:], q_decs[h][rows]], axis=0), states[h])
            v_new = uws[h][rows, :dk] - ws[:c]
            o = ws[c:] + _bdot(qks[h][rows, n * c:(n + 1) * c], v_new)
            states[h] = states[h] * jnp.exp(gc_last) + _bdot_tn(k_dec, v_new)
            zg = _silu(z_ref[0, rows, h * dk:(h + 1) * dk].astype(F32))
            o = o * lax.rsqrt(jnp.mean(o * o, axis=-1, keepdims=True) + EPS) * nw * zg
            o_ref[0, rows, h * dk:(h + 1) * dk] = o.astype(o_ref.dtype)
    for h in heads:
        s_ref[h] = states[h]


def _gdn_level_masks(ts):
    i = jnp.arange(ts)[:, None]
    j = jnp.arange(ts)[None, :]
    masks = []
    s = 1
    while s < GDN_CHUNK:
        masks.append(((i // (2 * s)) == (j // (2 * s))) & ((i // s) != (j // s)))
        s *= 2
    return jnp.stack(masks).astype(BF16)


def _lru_body(xb_ref, yb_ref, cw_ref, cb_ref, wa_ref, ba_ref, wx_ref, bx_ref, lam_ref,
              o_ref, xbuf, carry_ref, cols):
    ts = xb_ref.shape[1]
    xc = _causal_conv(xb_ref[0, :, cols].astype(F32), cw_ref, xbuf, cols) + cb_ref[:, cols]
    r = _sigmoid(_bdot(xc, wa_ref[cols, cols]) + ba_ref[:, cols])
    i = _sigmoid(_bdot(xc, wx_ref[cols, cols]) + bx_ref[:, cols])
    log_a = (-LRU_C) * r * _softplus(-lam_ref[:, cols])
    a = jnp.exp(log_a)
    mult = jnp.sqrt(jnp.maximum(1.0 - jnp.exp(2.0 * log_a), 0.0))
    b = mult * (i * xc)
    row = lax.broadcasted_iota(jnp.int32, a.shape, 0)
    d = 1
    while d < ts:
        keep = row >= d
        a_s = jnp.where(keep, pltpu.roll(a, d, axis=0), 1.0)
        b_s = jnp.where(keep, pltpu.roll(b, d, axis=0), 0.0)
        b = a * b_s + b
        a = a * a_s
        d *= 2
    h = a * carry_ref[:, cols] + b
    carry_ref[:, cols] = h[ts - 1:ts, :]
    o_ref[0, :, cols] = (h * _gelu_tanh(yb_ref[0, :, cols].astype(F32))).astype(o_ref.dtype)


N_GDN_IN, N_LRU_IN = 9, 9


def _mixers_kernel(*refs):
    gdn_in = refs[:N_GDN_IN]
    lru_in = refs[N_GDN_IN:N_GDN_IN + N_LRU_IN]
    o_gdn_ref, o_lru_ref, gdn_xbuf, s_ref, lru_xbuf, carry_ref = refs[N_GDN_IN + N_LRU_IN:]

    @pl.when(pl.program_id(1) == 0)
    def _():
        s_ref[...] = jnp.zeros(s_ref.shape, F32)
        carry_ref[...] = jnp.zeros(carry_ref.shape, F32)
        gdn_xbuf[0:HALO, :] = jnp.zeros((HALO, gdn_xbuf.shape[1]), F32)
        lru_xbuf[0:HALO, :] = jnp.zeros((HALO, lru_xbuf.shape[1]), F32)

    groups = [slice(g * LRU_GROUP, (g + 1) * LRU_GROUP) for g in range(LRU_WIDTH // LRU_GROUP)]
    _gdn_body(*gdn_in, o_gdn_ref, gdn_xbuf, s_ref,
              fillers=[functools.partial(_lru_body, *lru_in, o_lru_ref, lru_xbuf, carry_ref, cols)
                       for cols in groups])


def _mixers(gdn_args, lru_args, ts=256):
    qkv, z, ba, bat = gdn_args[:4]
    xb, yb = lru_args[:2]
    bsz, s, _ = qkv.shape
    lvl = _gdn_level_masks(ts)
    gdn_args = tuple(gdn_args) + (lvl,)
    assert len(gdn_args) == N_GDN_IN and len(lru_args) == N_LRU_IN
    tok = lambda n: pl.BlockSpec((1, ts, n), lambda b, i: (b, i, 0))
    full = lambda a: pl.BlockSpec(a.shape, lambda b, i: (0,) * a.ndim)
    in_specs = ([tok(3 * GDN_WIDTH), tok(GDN_WIDTH), tok(8), pl.BlockSpec((1, 8, ts), lambda b, i: (b, 0, i))]
                + [full(a) for a in gdn_args[4:]]
                + [tok(LRU_WIDTH), tok(LRU_WIDTH)] + [full(a) for a in lru_args[2:]])
    return pl.pallas_call(
        _mixers_kernel,
        grid=(bsz, s // ts),
        in_specs=in_specs,
        out_specs=[tok(GDN_WIDTH), tok(LRU_WIDTH)],
        out_shape=[jax.ShapeDtypeStruct((bsz, s, GDN_WIDTH), BF16),
                   jax.ShapeDtypeStruct((bsz, s, LRU_WIDTH), BF16)],
        scratch_shapes=[pltpu.VMEM((ts + HALO, 3 * GDN_WIDTH), F32),
                        pltpu.VMEM((GDN_HEADS, GDN_HEAD_DIM, GDN_HEAD_DIM), F32),
                        pltpu.VMEM((ts + HALO, LRU_WIDTH), F32), pltpu.VMEM((1, LRU_WIDTH), F32)],
        compiler_params=_cparams(("arbitrary", "arbitrary")),
        name="mixers",
    )(*gdn_args, *lru_args)


def _merge_kernel(x_ref, og_ref, ol_ref, gg_ref, gl_ref, mod_ref, nw_ref,
                  wg_ref, wl_ref, wo_ref, wrt_ref, br_ref,
                  x1_ref, h2_ref, idx_ref, gate_ref, rank_ref, cnt_ref, carry_ref, hprev_ref):
    tm = x_ref.shape[1]
    step = pl.program_id(0)
    n_tiles = pl.num_programs(0) - 1

    @pl.when(step == 0)
    def _():
        carry_ref[...] = jnp.zeros(carry_ref.shape, F32)

    def mix_branches():
        return (_sigmoid(gg_ref[0].astype(F32)) * jnp.dot(og_ref[0], wg_ref[...], preferred_element_type=F32)
                + _sigmoid(gl_ref[0].astype(F32)) * jnp.dot(ol_ref[0], wl_ref[...], preferred_element_type=F32))

    def mix_out(merged):
        x1 = x_ref[0] + mod_ref[0, 2:3, :] * _bdot(merged, wo_ref[...])
        x1_ref[0] = x1
        return x1

    def mix_norm(x1):
        y = x1 * lax.rsqrt(jnp.mean(x1 * x1, axis=-1, keepdims=True) + EPS)
        h2 = (y * nw_ref[...]) * (1.0 + mod_ref[0, 4:5, :]) + mod_ref[0, 3:4, :]
        h2_ref[...] = h2.reshape(tm, 1, h2.shape[-1])
        hprev_ref[...] = h2

    def route_logits():
        return lax.dot_general(wrt_ref[...], hprev_ref[...], (((1,), (1,)), ((), ())),
                               preferred_element_type=F32, precision=lax.Precision.HIGHEST) + br_ref[...]

    def route_topk(logits):
        e_iota = lax.broadcasted_iota(jnp.int32, logits.shape, 0)
        work = logits
        sel = jnp.zeros(logits.shape, F32)
        vals, idxs = [], []
        for _ in range(TOP_K):
            m = jnp.max(work, axis=0, keepdims=True)
            idx = jnp.min(jnp.where(work == m, e_iota, N_EXPERTS), axis=0, keepdims=True)
            hit = e_iota == idx
            sel = jnp.where(hit, 1.0, sel)
            work = jnp.where(hit, -jnp.inf, work)
            vals.append(m)
            idxs.append(idx)
        return e_iota, sel, vals, idxs

    def route_emit(e_iota, sel, vals, idxs):
        exps = [jnp.exp(v - vals[0]) for v in vals]
        denom = exps[0] + exps[1] + exps[2] + exps[3]
        r_i = lax.broadcasted_iota(jnp.int32, (tm, tm), 0)
        c_i = lax.broadcasted_iota(jnp.int32, (tm, tm), 1)
        before = jnp.where(r_i < c_i, 1.0, 0.0).astype(BF16)
        prefix = jnp.dot(sel.astype(BF16), before, preferred_element_type=F32) + carry_ref[...]
        for kk in range(TOP_K):
            idx_ref[0, kk:kk + 1, :] = idxs[kk]
            gate_ref[0, kk:kk + 1, :] = exps[kk] / denom
            rank = jnp.sum(jnp.where(e_iota == idxs[kk], prefix, 0.0), axis=0, keepdims=True)
            rank_ref[0, kk:kk + 1, :] = rank.astype(jnp.int32)
        total = carry_ref[...] + jnp.sum(sel, axis=1, keepdims=True)
        carry_ref[...] = total
        cnt_ref[...] = total.astype(jnp.int32)

    @pl.when(step == 0)
    def _():
        mix_norm(mix_out(mix_branches()))

    @pl.when((step > 0) & (step < n_tiles))
    def _():
        logits = route_logits()
        merged = mix_branches()
        picked = route_topk(logits)
        x1 = mix_out(merged)
        route_emit(*picked)
        mix_norm(x1)

    @pl.when(step == n_tiles)
    def _():
        route_emit(*route_topk(route_logits()))


def _merge(x, og, ol, gg, gl, mod, norm_w, wg, wl, wo, wrt, br, tm=256):
    bsz, s, d = x.shape
    ns = s // tm
    nt = bsz * ns
    cur = lambda t: jnp.minimum(t, nt - 1)
    tok = lambda n: pl.BlockSpec((1, tm, n), lambda t: (cur(t) // ns, cur(t) % ns, 0))
    full = lambda a: pl.BlockSpec(a.shape, lambda t: (0,) * a.ndim)
    lane = pl.BlockSpec((1, TOP_K, tm), lambda t: (jnp.maximum(t - 1, 0), 0, 0))
    return pl.pallas_call(
        _merge_kernel,
        grid=(nt + 1,),
        in_specs=[tok(d), tok(GDN_WIDTH), tok(LRU_WIDTH), tok(d), tok(d),
                  pl.BlockSpec((1, 6, d), lambda t: (cur(t) // ns, 0, 0)),
                  full(norm_w), full(wg), full(wl), full(wo), full(wrt), full(br)],
        out_specs=[tok(d),
                   pl.BlockSpec((tm, 1, d), lambda t: (cur(t), 0, 0)),
                   lane, lane, lane,
                   pl.BlockSpec((N_EXPERTS, 1), lambda t: (0, 0))],
        out_shape=[jax.ShapeDtypeStruct((bsz, s, d), F32),
                   jax.ShapeDtypeStruct((bsz * s, 1, d), F32),
                   jax.ShapeDtypeStruct((nt, TOP_K, tm), jnp.int32),
                   jax.ShapeDtypeStruct((nt, TOP_K, tm), F32),
                   jax.ShapeDtypeStruct((nt, TOP_K, tm), jnp.int32),
                   jax.ShapeDtypeStruct((N_EXPERTS, 1), jnp.int32)],
        scratch_shapes=[pltpu.VMEM((N_EXPERTS, 1), F32), pltpu.VMEM((tm, d), F32)],
        compiler_params=_cparams(("arbitrary",)),
        name="merge",
    )(x, og, ol, gg, gl, mod, norm_w, wg, wl, wo, wrt, br)


def _dispatch_kernel(dest_ref, zinfo_ref, h_ref, xs_ref, zbuf, sem, zsem):
    tm = h_ref.shape[0]
    n_blocks = xs_ref.shape[0] // MOE_BLOCK

    def row_copy(r, slot):
        return pltpu.make_async_copy(h_ref.at[r], xs_ref.at[slot], sem)

    def issue(r, carry):
        for kk in range(TOP_K):
            row_copy(r, dest_ref[0, 0, kk * tm + r]).start()
        return carry

    lax.fori_loop(0, tm, issue, 0)

    @pl.when(pl.program_id(0) == 0)
    def _():
        zbuf[...] = jnp.zeros(zbuf.shape, F32)

        def zero_copy(start, rows):
            return pltpu.make_async_copy(zbuf.at[pl.ds(0, rows)], xs_ref.at[pl.ds(start, rows)], zsem)

        def pad_runs(wait):
            def per_expert(e, carry):
                start = zinfo_ref[0, e]
                length = zinfo_ref[0, N_EXPERTS + e]
                rows = MOE_BLOCK // 2
                while rows >= 1:
                    @pl.when((length & rows) != 0)
                    def _(rows=rows):
                        cp = zero_copy(start + (length & (-2 * rows)), rows)
                        cp.wait() if wait else cp.start()
                    rows //= 2
                return carry

            lax.fori_loop(0, N_EXPERTS, per_expert, 0)

            def per_block(b, carry):
                cp = zero_copy(b * MOE_BLOCK, MOE_BLOCK)
                cp.wait() if wait else cp.start()
                return carry

            lax.fori_loop(zinfo_ref[0, 2 * N_EXPERTS], n_blocks, per_block, 0)

        pad_runs(False)
        pad_runs(True)

    def wait(r, carry):
        for kk in range(TOP_K):
            row_copy(r, 0).wait()
        return carry

    lax.fori_loop(0, tm, wait, 0)


def _dispatch(h2_rows, dest, zinfo, n_slots, tm=256):
    n_tok, _, d = h2_rows.shape
    nt = n_tok // tm
    return pl.pallas_call(
        _dispatch_kernel,
        grid=(nt,),
        in_specs=[pl.BlockSpec((1, 1, TOP_K * tm), lambda i: (i, 0, 0), memory_space=pltpu.SMEM),
                  pl.BlockSpec(zinfo.shape, lambda i: (0, 0), memory_space=pltpu.SMEM),
                  pl.BlockSpec((tm, 1, d), lambda i: (i, 0, 0))],
        out_specs=pl.BlockSpec(memory_space=pl.ANY),
        out_shape=jax.ShapeDtypeStruct((n_slots, 1, d), F32),
        scratch_shapes=[pltpu.VMEM((MOE_BLOCK, 1, d), F32), pltpu.SemaphoreType.DMA,
                        pltpu.SemaphoreType.DMA],
        compiler_params=_cparams(("arbitrary",)),
        name="dispatch",
    )(dest, zinfo, h2_rows)


def _expert_kernel(be_ref, nu_ref, win_ref, nval_ref, spare_ref, src_ref,
                   xs_ref, w1_ref, b1g_ref, b1l_ref, w2_ref, b2_ref, perm_ref,
                   yt_ref, x2d, w1p, w2b, yb0, yb1, yb2, sems):
    blk, _, d = xs_ref.shape
    f = w2b.shape[0]
    n_groups = 2 * f // EXPERT_COLS
    half = EXPERT_COLS // 2
    b = pl.program_id(0)
    n_blocks = pl.num_programs(0) - 1
    bufs = (yb0, yb1, yb2)
    live = b < nu_ref[0]
    b_w = jnp.minimum(b, n_blocks - 1)
    fresh = (b == 0) | (be_ref[b_w] != be_ref[jnp.maximum(b_w - 1, 0)])

    b_prev = jnp.maximum(b - 1, 0)
    win, n_val, spare = win_ref[b_prev], nval_ref[b_prev], spare_ref[b_prev]

    def row_copy(buf, sem, j):
        dst = jnp.where(j < n_val, src_ref[win + j], spare + j)
        return pltpu.make_async_copy(buf.at[j], yt_ref.at[dst], sem)

    def tile_wait(buf, sem):
        pltpu.make_async_copy(buf, yt_ref.at[pl.ds(0, blk)], sem).wait()

    def ffn(out_buf):
        x2d[...] = xs_ref[...].reshape(blk, d)
        u = jnp.dot(x2d[...].astype(BF16), w1p[...], preferred_element_type=F32)
        acts = []
        for j in range(n_groups):
            u_glu = u[:, j * EXPERT_COLS:j * EXPERT_COLS + half] + b1g_ref[0, :, j * half:(j + 1) * half]
            u_lin = u[:, j * EXPERT_COLS + half:(j + 1) * EXPERT_COLS] + b1l_ref[0, :, j * half:(j + 1) * half]
            u_glu = jnp.minimum(u_glu, SWIGLU_LIMIT)
            u_lin = jnp.clip(u_lin, -SWIGLU_LIMIT, SWIGLU_LIMIT)
            acts.append((u_glu * _sigmoid(SWIGLU_ALPHA * u_glu) * (u_lin + 1.0)).astype(BF16))
        act = jnp.concatenate(acts, axis=1)
        y = jnp.dot(act, w2b[...], preferred_element_type=F32) + b2_ref[0]
        out_buf[...] = y.reshape(blk, 1, d)

    @pl.when(live & fresh)
    def _():
        for j in range(n_groups):
            cols = slice(j * EXPERT_COLS, (j + 1) * EXPERT_COLS)
            w1p[:, cols] = jnp.dot(w1_ref[0, :, cols].astype(BF16), perm_ref[...],
                                   preferred_element_type=F32).astype(BF16)
        w2b[...] = w2_ref[0].astype(BF16)

    for p in range(3):
        cur, prv = bufs[p], bufs[(p + 2) % 3]
        sem_cur, sem_prv, sem_old = sems.at[p], sems.at[(p + 2) % 3], sems.at[(p + 1) % 3]
        mine = b % 3 == p

        @pl.when(mine & (b >= 3))
        def _(cur=cur, sem_cur=sem_cur):
            tile_wait(cur, sem_cur)

        @pl.when(mine & (b >= 1) & live)
        def _(cur=cur, prv=prv, sem_prv=sem_prv):
            for j in range(blk):
                row_copy(prv, sem_prv, j).start()
            ffn(cur)

        @pl.when(mine & ((b == 0) | jnp.logical_not(live)))
        def _(cur=cur, prv=prv, old=bufs[(p + 1) % 3], sem_prv=sem_prv, sem_old=sem_old):
            @pl.when(b >= 1)
            def _():
                def issue(j, carry):
                    row_copy(prv, sem_prv, j).start()
                    return carry

                lax.fori_loop(0, blk, issue, 0)

            @pl.when(live)
            def _():
                ffn(cur)

            @pl.when(jnp.logical_not(live) & (b < n_blocks))
            def _():
                cur[...] = jnp.zeros(cur.shape, F32)

            @pl.when(b == n_blocks)
            def _():
                tile_wait(old, sem_old)
                tile_wait(prv, sem_prv)


def _pair_split_matrix():
    i = jnp.arange(EXPERT_COLS)[:, None]
    j = jnp.arange(EXPERT_COLS)[None, :]
    half = EXPERT_COLS // 2
    return jnp.where(j < half, i == 2 * j, i == 2 * (j - half) + 1).astype(BF16)


def _experts(block_e, n_used, win_start, n_valid, spare0, src_sorted, xs, w1, b1g, b1l, w2, b2, n_rows_out):
    n_slots, _, d = xs.shape
    f = w2.shape[1]
    n_blocks = n_slots // MOE_BLOCK
    perm = _pair_split_matrix()
    slot = lambda b, be, nu, *_: (jnp.minimum(b, nu[0] - 1), 0, 0)
    wsel = lambda b, be, nu, *_: (be[jnp.minimum(b, n_blocks - 1)], 0, 0)
    row_buf = pltpu.VMEM((MOE_BLOCK, 1, d), F32)
    grid_spec = pltpu.PrefetchScalarGridSpec(
        num_scalar_prefetch=6,
        grid=(n_blocks + 1,),
        in_specs=[pl.BlockSpec((MOE_BLOCK, 1, d), slot),
                  pl.BlockSpec((1, d, 2 * f), wsel),
                  pl.BlockSpec((1, 1, f), wsel), pl.BlockSpec((1, 1, f), wsel),
                  pl.BlockSpec((1, f, d), wsel), pl.BlockSpec((1, 1, d), wsel),
                  pl.BlockSpec(perm.shape, lambda b, *_: (0, 0))],
        out_specs=pl.BlockSpec(memory_space=pl.ANY),
        scratch_shapes=[pltpu.VMEM((MOE_BLOCK, d), F32), pltpu.VMEM((d, 2 * f), BF16),
                        pltpu.VMEM((f, d), BF16), row_buf, row_buf, row_buf,
                        pltpu.SemaphoreType.DMA((3,))],
    )
    return pl.pallas_call(
        _expert_kernel,
        grid_spec=grid_spec,
        out_shape=jax.ShapeDtypeStruct((n_rows_out, 1, d), F32),
        compiler_params=pltpu.CompilerParams(dimension_semantics=("arbitrary",),
                                             vmem_limit_bytes=EXPERT_VMEM_LIMIT),
        name="experts",
    )(block_e, n_used, win_start, n_valid, spare0, src_sorted, xs, w1, b1g, b1l, w2, b2, perm)


def _combine_kernel(x1_ref, gate_ref, mod_ref, nw_ref, yt_ref, o_ref, buf2):
    tm = x1_ref.shape[1]
    d = x1_ref.shape[2]
    buf2[...] = yt_ref[...].reshape(TOP_K * tm, d)
    gate = gate_ref[0]
    acc = None
    for kk in range(TOP_K):
        term = gate[:, kk:kk + 1] * buf2[kk * tm:(kk + 1) * tm, :]
        acc = term if acc is None else acc + term
    x2 = x1_ref[0] + mod_ref[0, 5:6, :] * acc
    o_ref[0] = x2 * lax.rsqrt(jnp.mean(x2 * x2, axis=-1, keepdims=True) + EPS) * nw_ref[...]


def _combine(x1, gate, mod, norm_w, yt, tm=256):
    bsz, s, d = x1.shape
    ns = s // tm
    tok = lambda n: pl.BlockSpec((1, tm, n), lambda b, i: (b, i, 0))
    return pl.pallas_call(
        _combine_kernel,
        grid=(bsz, ns),
        in_specs=[tok(d),
                  pl.BlockSpec((1, tm, TOP_K), lambda b, i: (b * ns + i, 0, 0)),
                  pl.BlockSpec((1, 6, d), lambda b, i: (b, 0, 0)),
                  pl.BlockSpec((1, d), lambda b, i: (0, 0)),
                  pl.BlockSpec((TOP_K * tm, 1, d), lambda b, i: (b * ns + i, 0, 0))],
        out_specs=tok(d),
        out_shape=jax.ShapeDtypeStruct((bsz, s, d), F32),
        scratch_shapes=[pltpu.VMEM((TOP_K * tm, d), F32)],
        compiler_params=_cparams(("arbitrary", "arbitrary")),
        name="combine",
    )(x1, gate, mod, norm_w, yt)


def _block_diag(w):
    nb, bi, bo = w.shape
    eye = jnp.eye(nb, dtype=w.dtype)
    return (eye[:, None, :, None] * w[:, :, None, :]).reshape(nb * bi, nb * bo)


def _layer(x, c, w_ada, b_ada, norm_mix, norm_ffn, w_in, gdn_conv_w, gdn_a_log, gdn_dt_bias,
           gdn_norm_w, lru_conv_w, lru_conv_b, lru_w_a, lru_b_a, lru_w_x, lru_b_x, lru_lambda,
           w_branch_gdn, w_branch_lru, w_out, w_router, b_router, w1, b1, w2, b2, norm_final):
    bsz, s, d = x.shape
    n_tok = bsz * s
    tm = 256
    row = lambda v: v.reshape(1, -1).astype(F32)

    mod = _ada(c, w_ada, b_ada).reshape(bsz, 6, d)

    o_ba = 4 * GDN_WIDTH
    w_main = jnp.concatenate([w_in[:, :o_ba], w_in[:, o_ba + 2 * GDN_HEADS:]], axis=1).astype(BF16)
    w_ba_t = jnp.pad(w_in[:, o_ba:o_ba + 2 * GDN_HEADS].T, ((0, 128 - 2 * GDN_HEADS), (0, 0))).astype(BF16)
    qkv, z, xb, yb, gg, gl, ba, bat = _inproj(x, mod, row(norm_mix), w_main, w_ba_t)

    zeros4 = jnp.zeros((GDN_HEADS,), F32)
    hvals = jnp.stack([jnp.concatenate([zeros4, gdn_a_log.astype(F32)]),
                       jnp.concatenate([zeros4, gdn_dt_bias.astype(F32)])])
    o_gdn, o_lru = _mixers(
        (qkv, z, ba, bat, gdn_conv_w.astype(F32), hvals, hvals.T, row(gdn_norm_w)),
        (xb, yb, lru_conv_w.astype(F32), row(lru_conv_b),
         _block_diag(lru_w_a).astype(BF16), row(lru_b_a),
         _block_diag(lru_w_x).astype(BF16), row(lru_b_x), row(lru_lambda)))

    x1, h2_rows, idx_t, gate_t, rank_t, counts = _merge(
        x, o_gdn, o_lru, gg, gl, mod, row(norm_ffn),
        w_branch_gdn.astype(BF16), w_branch_lru.astype(BF16), w_out.astype(BF16),
        w_router.T.astype(F32), b_router.reshape(-1, 1).astype(F32), tm=tm)

    counts = counts.reshape(N_EXPERTS)
    padded = (counts + MOE_BLOCK - 1) // MOE_BLOCK * MOE_BLOCK
    pend = jnp.cumsum(padded)
    pstart = pend - padded
    assert (n_tok * TOP_K) % MOE_BLOCK == 0 and s % tm == 0
    n_blocks = (n_tok * TOP_K) // MOE_BLOCK + N_EXPERTS
    n_slots = n_blocks * MOE_BLOCK
    block_first = jnp.arange(n_blocks, dtype=jnp.int32) * MOE_BLOCK
    block_e = jnp.minimum(jnp.sum(pend[None, :] <= block_first[:, None], axis=1),
                          N_EXPERTS - 1).astype(jnp.int32)
    n_used = (pend[-1:] // MOE_BLOCK).astype(jnp.int32)
    onehot = idx_t[..., None] == jnp.arange(N_EXPERTS, dtype=jnp.int32)
    dest = (rank_t + jnp.sum(jnp.where(onehot, pstart.astype(jnp.int32), 0), axis=-1)).astype(jnp.int32)
    dest = dest.reshape(n_tok // tm, 1, TOP_K * tm)
    zinfo = jnp.concatenate([pstart + counts, padded - counts, n_used,
                             jnp.zeros((128 - 2 * N_EXPERTS - 1,), jnp.int32)]).astype(jnp.int32)

    n_assign = n_tok * TOP_K
    _, src_sorted = lax.sort_key_val(dest.reshape(-1), jnp.arange(n_assign, dtype=jnp.int32))
    src_sorted = jnp.concatenate([src_sorted, jnp.zeros((MOE_BLOCK,), jnp.int32)])
    cstart = (jnp.cumsum(counts) - counts).astype(jnp.int32)
    seg_rank = block_first - pstart[block_e].astype(jnp.int32)
    n_valid = jnp.clip(counts[block_e].astype(jnp.int32) - seg_rank, 0, MOE_BLOCK).astype(jnp.int32)
    win_start = jnp.clip(cstart[block_e] + seg_rank, 0, n_assign).astype(jnp.int32)
    spare0 = (n_assign + block_first - cstart[block_e] - counts[block_e].astype(jnp.int32)).astype(jnp.int32)

    xs = _dispatch(h2_rows, dest, zinfo.reshape(1, 128), n_slots, tm=tm)
    yt = _experts(block_e, n_used, win_start, n_valid, spare0, src_sorted, xs, w1.astype(F32),
                  b1[:, None, 0::2].astype(F32), b1[:, None, 1::2].astype(F32),
                  w2.astype(F32), b2[:, None, :].astype(F32), n_assign + N_EXPERTS * MOE_BLOCK)
    gate = jnp.swapaxes(gate_t, 1, 2)
    return _combine(x1, gate, mod, row(norm_final), yt, tm=tm)


def kernel(x, c, w_ada, b_ada, norm_mix, norm_ffn, w_in, gdn_conv_w, gdn_a_log, gdn_dt_bias,
           gdn_norm_w, lru_conv_w, lru_conv_b, lru_w_a, lru_b_a, lru_w_x, lru_b_x, lru_lambda,
           w_branch_gdn, w_branch_lru, w_out, w_router, b_router, w1, b1, w2, b2, norm_final):
    assert w_ada.shape[0] == 1, "single-layer block"
    return _layer(x, c, w_ada[0], b_ada[0], norm_mix[0], norm_ffn[0], w_in[0], gdn_conv_w[0],
                  gdn_a_log[0], gdn_dt_bias[0], gdn_norm_w[0], lru_conv_w[0], lru_conv_b[0],
                  lru_w_a[0], lru_b_a[0], lru_w_x[0], lru_b_x[0], lru_lambda[0],
                  w_branch_gdn[0], w_branch_lru[0], w_out[0], w_router[0], b_router[0],
                  w1[0], b1[0], w2[0], b2[0], norm_final)
```

```python
import functools

import jax
import jax.numpy as jnp
from jax import lax
from jax.experimental import pallas as pl
from jax.experimental.pallas import tpu as pltpu

F32 = jnp.float32
BF16 = jnp.bfloat16

EPS = 1e-6
GDN_HEADS = 4
GDN_HEAD_DIM = 128
GDN_WIDTH = GDN_HEADS * GDN_HEAD_DIM
GDN_CHUNK = 64
LRU_WIDTH = 512
LRU_C = 8.0
LRU_GROUP = 128
CONV_WIDTH = 4
N_EXPERTS = 32
TOP_K = 4
SWIGLU_ALPHA = 1.702
SWIGLU_LIMIT = 7.0
MOE_BLOCK = 512
EXPERT_COLS = 256

HALO = 8
VMEM_LIMIT = 56 * 1024 * 1024
EXPERT_VMEM_LIMIT = 60 * 1024 * 1024


def _cparams(sem):
    return pltpu.CompilerParams(dimension_semantics=sem, vmem_limit_bytes=VMEM_LIMIT)


def _sigmoid(x):
    return 1.0 / (1.0 + jnp.exp(-x))


def _silu(x):
    return x * _sigmoid(x)


def _softplus(x):
    return jnp.maximum(x, 0.0) + jnp.log1p(jnp.exp(-jnp.abs(x)))


def _gelu_tanh(x):
    return 0.5 * x * (1.0 + jnp.tanh(0.7978845608028654 * (x + 0.044715 * (x * x * x))))


def _bdot(a, b):
    return jnp.dot(a.astype(BF16), b.astype(BF16), preferred_element_type=F32)


def _bdot_nt(a, b):
    return lax.dot_general(a.astype(BF16), b.astype(BF16), (((1,), (1,)), ((), ())),
                           preferred_element_type=F32)


def _bdot_tn(a, b):
    return lax.dot_general(a.astype(BF16), b.astype(BF16), (((0,), (0,)), ((), ())),
                           preferred_element_type=F32)


def _fdot(a, b):
    return jnp.dot(a, b, preferred_element_type=F32, precision=lax.Precision.HIGHEST)


def _ada_kernel(ct_ref, w_ref, b_ref, o_ref):
    ct = _silu(ct_ref[...])
    w = w_ref[...]
    for b in range(ct.shape[1]):
        o_ref[b:b + 1, :] = jnp.sum(ct[:, b:b + 1] * w, axis=0, keepdims=True) + b_ref[...]


def _ada(c, w_ada, b_ada):
    bsz, d = c.shape
    n = w_ada.shape[1]
    tn = 512
    return pl.pallas_call(
        _ada_kernel,
        grid=(n // tn,),
        in_specs=[pl.BlockSpec((d, bsz), lambda j: (0, 0)),
                  pl.BlockSpec((d, tn), lambda j: (0, j)),
                  pl.BlockSpec((1, tn), lambda j: (0, j))],
        out_specs=pl.BlockSpec((bsz, tn), lambda j: (0, j)),
        out_shape=jax.ShapeDtypeStruct((bsz, n), F32),
        compiler_params=_cparams(("arbitrary",)),
        name="ada",
    )(c.T, w_ada, b_ada.reshape(1, n))


def _inproj_kernel(x_ref, mod_ref, nw_ref, wm_ref, wba_ref,
                   qkv_ref, z_ref, xb_ref, yb_ref, gg_ref, gl_ref, ba_ref, bat_ref):
    x = x_ref[0]
    shift = mod_ref[0, 0:1, :]
    scale = mod_ref[0, 1:2, :]
    y = x * lax.rsqrt(jnp.mean(x * x, axis=-1, keepdims=True) + EPS)
    h = ((y * nw_ref[...]) * (1.0 + scale) + shift).astype(BF16)
    col = 0
    for ref in (qkv_ref, z_ref, xb_ref, yb_ref, gg_ref, gl_ref):
        n = ref.shape[-1]
        ref[0] = jnp.dot(h, wm_ref[:, col:col + n], preferred_element_type=F32).astype(ref.dtype)
        col += n
    wba = wba_ref[...]
    nt = (((1,), (1,)), ((), ()))
    bat_ref[0] = lax.dot_general(wba[0:8], h, nt, preferred_element_type=F32)
    ba_ref[0] = lax.dot_general(h, wba, nt, preferred_element_type=F32)[:, 0:8]


def _inproj(x, mod, norm_w, w_main, w_ba_t, ts=512):
    bsz, s, d = x.shape
    widths = (3 * GDN_WIDTH, GDN_WIDTH, LRU_WIDTH, LRU_WIDTH, d, d)
    tok = lambda n: pl.BlockSpec((1, ts, n), lambda b, i: (b, i, 0))
    out_shapes = [jax.ShapeDtypeStruct((bsz, s, n), BF16) for n in widths]
    out_shapes += [jax.ShapeDtypeStruct((bsz, s, 8), F32), jax.ShapeDtypeStruct((bsz, 8, s), F32)]
    out_specs = [tok(n) for n in widths]
    out_specs += [tok(8), pl.BlockSpec((1, 8, ts), lambda b, i: (b, 0, i))]
    return pl.pallas_call(
        _inproj_kernel,
        grid=(bsz, s // ts),
        in_specs=[tok(d),
                  pl.BlockSpec((1, 6, d), lambda b, i: (b, 0, 0)),
                  pl.BlockSpec((1, d), lambda b, i: (0, 0)),
                  pl.BlockSpec(w_main.shape, lambda b, i: (0, 0)),
                  pl.BlockSpec(w_ba_t.shape, lambda b, i: (0, 0))],
        out_specs=out_specs,
        out_shape=out_shapes,
        compiler_params=_cparams(("arbitrary", "arbitrary")),
        name="inproj",
    )(x, mod, norm_w, w_main, w_ba_t)


def _causal_conv(x_tile, w_ref, xbuf, cols=slice(None)):
    ts = x_tile.shape[0]
    xbuf[HALO:HALO + ts, cols] = x_tile
    acc = None
    for j in range(CONV_WIDTH):
        off = HALO - (CONV_WIDTH - 1) + j
        term = w_ref[j:j + 1, cols] * xbuf[off:off + ts, cols]
        acc = term if acc is None else acc + term
    xbuf[0:HALO, cols] = xbuf[ts:ts + HALO, cols]
    return acc


def _gdn_body(qkv_ref, z_ref, ba_ref, bat_ref, cw_ref, hrow_ref, hcol_ref, nw_ref, lvl_ref,
              o_ref, xbuf, s_ref, fillers=()):
    ts = qkv_ref.shape[1]
    c = GDN_CHUNK
    dk = GDN_HEAD_DIM
    qkv = _silu(_causal_conv(qkv_ref[0].astype(F32), cw_ref, xbuf))

    ba = ba_ref[0]
    g_col = -jnp.exp(hrow_ref[0:1, :]) * _softplus(ba + hrow_ref[1:2, :])
    beta_col = _sigmoid(ba)
    r_i = lax.broadcasted_iota(jnp.int32, (ts, ts), 0)
    c_i = lax.broadcasted_iota(jnp.int32, (ts, ts), 1)
    same = (r_i // c) == (c_i // c)
    incl = same & (c_i <= r_i)
    strict = same & (c_i < r_i)
    gc_col = _fdot(jnp.where(incl, 1.0, 0.0).astype(F32), g_col)
    g_row = -jnp.exp(hcol_ref[:, 0:1]) * _softplus(bat_ref[0] + hcol_ref[:, 1:2])
    gc_row = _fdot(g_row, jnp.where(same & (r_i <= c_i), 1.0, 0.0).astype(F32))
    eye = jnp.where(r_i == c_i, 1.0, 0.0).astype(F32)
    scale = dk ** -0.5
    nw = nw_ref[...]
    n_lvl = lvl_ref.shape[0]

    heads = range(GDN_HEADS)
    ks, gccs, a_bf, t_mat, qks, rhs, q_decs = [], [], [], [], [], [], []
    for h in heads:
        q = qkv[:, h * dk:(h + 1) * dk]
        k = qkv[:, GDN_WIDTH + h * dk:GDN_WIDTH + (h + 1) * dk]
        v = qkv[:, 2 * GDN_WIDTH + h * dk:2 * GDN_WIDTH + (h + 1) * dk]
        q = q * (lax.rsqrt(jnp.sum(q * q, axis=-1, keepdims=True) + EPS) * scale)
        k = k * lax.rsqrt(jnp.sum(k * k, axis=-1, keepdims=True) + EPS)
        beta = beta_col[:, h:h + 1]
        gcc = gc_col[:, GDN_HEADS + h:GDN_HEADS + h + 1]
        gcr = gc_row[GDN_HEADS + h:GDN_HEADS + h + 1, :]
        decay = jnp.exp(jnp.where(incl, gcc - gcr, 0.0))
        kb = k * beta
        m1 = _bdot_nt(jnp.concatenate([kb, q], axis=0), k)
        a_mat = jnp.where(strict, m1[:ts] * decay, 0.0)
        e_gc = jnp.exp(gcc)
        ks.append(k)
        gccs.append(gcc)
        qks.append(jnp.where(incl, m1[ts:] * decay, 0.0))
        a_bf.append(a_mat.astype(BF16))
        t_mat.append(eye - a_mat * lvl_ref[0].astype(F32))
        rhs.append(jnp.concatenate([v * beta, kb * e_gc], axis=1))
        q_decs.append(q * e_gc)
    for lv in range(1, n_lvl):
        t_bf = [t_mat[h].astype(BF16) for h in heads]
        inner = [_bdot(a_bf[h] * lvl_ref[lv], t_bf[h]) for h in heads]
        t_mat = [t_mat[h] - _bdot(t_bf[h], inner[h]) for h in heads]
    uws = [_bdot(t_mat[h], rhs[h]) for h in heads]
    states = [s_ref[h] for h in heads]
    pre = {}
    for n in range(ts // c):
        rows = slice(n * c, (n + 1) * c)
        for h in heads:
            gc_last = gccs[h][(n + 1) * c - 1:(n + 1) * c, :]
            k_dec = ks[h][rows] * jnp.exp(gc_last - gccs[h][rows])
            uw = uws[h][rows]
            qk_uw = _bdot(qks[h][rows, n * c:(n + 1) * c], uw)
            kd_uw = _bdot_tn(k_dec, uw)
            lhs = jnp.concatenate([kd_uw[:, dk:], q_decs[h][rows] - qk_uw[:, dk:]], axis=0)
            pre[n, h] = (jnp.exp(gc_last), lhs, kd_uw[:, :dk], qk_uw[:, :dk])
    for n in range(ts // c):
        rows = slice(n * c, (n + 1) * c)
        if n < len(fillers):
            fillers[n]()
        for h in heads:
            g_last, lhs, kd_u, qk_u = pre[n, h]
            prod = _bdot(lhs, states[h])
            o = prod[dk:] + qk_u
            states[h] = states[h] * g_last - prod[:dk] + kd_u
            zg = _silu(z_ref[0, rows, h * dk:(h + 1) * dk].astype(F32))
            o = o * lax.rsqrt(jnp.mean(o * o, axis=-1, keepdims=True) + EPS) * nw * zg
            o_ref[0, rows, h * dk:(h + 1) * dk] = o.astype(o_ref.dtype)
    for h in heads:
        s_ref[h] = states[h]


def _gdn_level_masks(ts):
    i = jnp.arange(ts)[:, None]
    j = jnp.arange(ts)[None, :]
    masks = []
    s = 1
    while s < GDN_CHUNK:
        masks.append(((i // (2 * s)) == (j // (2 * s))) & ((i // s) != (j // s)))
        s *= 2
    return jnp.stack(masks).astype(BF16)


def _lru_body(xb_ref, yb_ref, cw_ref, cb_ref, wa_ref, ba_ref, wx_ref, bx_ref, lam_ref,
              o_ref, xbuf, carry_ref, cols):
    ts = xb_ref.shape[1]
    xc = _causal_conv(xb_ref[0, :, cols].astype(F32), cw_ref, xbuf, cols) + cb_ref[:, cols]
    r = _sigmoid(_bdot(xc, wa_ref[cols, cols]) + ba_ref[:, cols])
    i = _sigmoid(_bdot(xc, wx_ref[cols, cols]) + bx_ref[:, cols])
    log_a = (-LRU_C) * r * _softplus(-lam_ref[:, cols])
    a = jnp.exp(log_a)
    mult = jnp.sqrt(jnp.maximum(1.0 - jnp.exp(2.0 * log_a), 0.0))
    b = mult * (i * xc)
    row = lax.broadcasted_iota(jnp.int32, a.shape, 0)
    d = 1
    while d < ts:
        keep = row >= d
        a_s = jnp.where(keep, pltpu.roll(a, d, axis=0), 1.0)
        b_s = jnp.where(keep, pltpu.roll(b, d, axis=0), 0.0)
        b = a * b_s + b
        a = a * a_s
        d *= 2
    h = a * carry_ref[:, cols] + b
    carry_ref[:, cols] = h[ts - 1:ts, :]
    o_ref[0, :, cols] = (h * _gelu_tanh(yb_ref[0, :, cols].astype(F32))).astype(o_ref.dtype)


N_GDN_IN, N_LRU_IN = 9, 9


def _mixers_kernel(*refs):
    gdn_in = refs[:N_GDN_IN]
    lru_in = refs[N_GDN_IN:N_GDN_IN + N_LRU_IN]
    o_gdn_ref, o_lru_ref, gdn_xbuf, s_ref, lru_xbuf, carry_ref = refs[N_GDN_IN + N_LRU_IN:]

    @pl.when(pl.program_id(1) == 0)
    def _():
        s_ref[...] = jnp.zeros(s_ref.shape, F32)
        carry_ref[...] = jnp.zeros(carry_ref.shape, F32)
        gdn_xbuf[0:HALO, :] = jnp.zeros((HALO, gdn_xbuf.shape[1]), F32)
        lru_xbuf[0:HALO, :] = jnp.zeros((HALO, lru_xbuf.shape[1]), F32)

    groups = [slice(g * LRU_GROUP, (g + 1) * LRU_GROUP) for g in range(LRU_WIDTH // LRU_GROUP)]
    _gdn_body(*gdn_in, o_gdn_ref, gdn_xbuf, s_ref,
              fillers=[functools.partial(_lru_body, *lru_in, o_lru_ref, lru_xbuf, carry_ref, cols)
                       for cols in groups])


def _mixers(gdn_args, lru_args, ts=256):
    qkv, z, ba, bat = gdn_args[:4]
    xb, yb = lru_args[:2]
    bsz, s, _ = qkv.shape
    lvl = _gdn_level_masks(ts)
    gdn_args = tuple(gdn_args) + (lvl,)
    assert len(gdn_args) == N_GDN_IN and len(lru_args) == N_LRU_IN
    tok = lambda n: pl.BlockSpec((1, ts, n), lambda b, i: (b, i, 0))
    full = lambda a: pl.BlockSpec(a.shape, lambda b, i: (0,) * a.ndim)
    in_specs = ([tok(3 * GDN_WIDTH), tok(GDN_WIDTH), tok(8), pl.BlockSpec((1, 8, ts), lambda b, i: (b, 0, i))]
                + [full(a) for a in gdn_args[4:]]
                + [tok(LRU_WIDTH), tok(LRU_WIDTH)] + [full(a) for a in lru_args[2:]])
    return pl.pallas_call(
        _mixers_kernel,
        grid=(bsz, s // ts),
        in_specs=in_specs,
        out_specs=[tok(GDN_WIDTH), tok(LRU_WIDTH)],
        out_shape=[jax.ShapeDtypeStruct((bsz, s, GDN_WIDTH), BF16),
                   jax.ShapeDtypeStruct((bsz, s, LRU_WIDTH), BF16)],
        scratch_shapes=[pltpu.VMEM((ts + HALO, 3 * GDN_WIDTH), F32),
                        pltpu.VMEM((GDN_HEADS, GDN_HEAD_DIM, GDN_HEAD_DIM), F32),
                        pltpu.VMEM((ts + HALO, LRU_WIDTH), F32), pltpu.VMEM((1, LRU_WIDTH), F32)],
        compiler_params=_cparams(("arbitrary", "arbitrary")),
        name="mixers",
    )(*gdn_args, *lru_args)


def _merge_kernel(x_ref, og_ref, ol_ref, gg_ref, gl_ref, mod_ref, nw_ref,
                  wg_ref, wl_ref, wo_ref, wrt_ref, br_ref,
                  x1_ref, h2_ref, idx_ref, gate_ref, rank_ref, cnt_ref, carry_ref, hprev_ref):
    tm = x_ref.shape[1]
    step = pl.program_id(0)
    n_tiles = pl.num_programs(0) - 1

    @pl.when(step == 0)
    def _():
        carry_ref[...] = jnp.zeros(carry_ref.shape, F32)

    def mix_branches():
        return (_sigmoid(gg_ref[0].astype(F32)) * jnp.dot(og_ref[0], wg_ref[...], preferred_element_type=F32)
                + _sigmoid(gl_ref[0].astype(F32)) * jnp.dot(ol_ref[0], wl_ref[...], preferred_element_type=F32))

    def mix_out(merged):
        x1 = x_ref[0] + mod_ref[0, 2:3, :] * _bdot(merged, wo_ref[...])
        x1_ref[0] = x1
        return x1

    def mix_norm(x1):
        y = x1 * lax.rsqrt(jnp.mean(x1 * x1, axis=-1, keepdims=True) + EPS)
        h2 = (y * nw_ref[...]) * (1.0 + mod_ref[0, 4:5, :]) + mod_ref[0, 3:4, :]
        h2_ref[...] = h2.reshape(tm, 1, h2.shape[-1])
        hprev_ref[...] = h2

    def route_logits():
        return lax.dot_general(wrt_ref[...], hprev_ref[...], (((1,), (1,)), ((), ())),
                               preferred_element_type=F32, precision=lax.Precision.HIGHEST) + br_ref[...]

    def route_topk(logits):
        e_iota = lax.broadcasted_iota(jnp.int32, logits.shape, 0)
        work = logits
        sel = jnp.zeros(logits.shape, F32)
        vals, idxs = [], []
        for _ in range(TOP_K):
            m = jnp.max(work, axis=0, keepdims=True)
            idx = jnp.min(jnp.where(work == m, e_iota, N_EXPERTS), axis=0, keepdims=True)
            hit = e_iota == idx
            sel = jnp.where(hit, 1.0, sel)
            work = jnp.where(hit, -jnp.inf, work)
            vals.append(m)
            idxs.append(idx)
        return e_iota, sel, vals, idxs

    def route_emit(e_iota, sel, vals, idxs):
        exps = [jnp.exp(v - vals[0]) for v in vals]
        denom = exps[0] + exps[1] + exps[2] + exps[3]
        r_i = lax.broadcasted_iota(jnp.int32, (tm, tm), 0)
        c_i = lax.broadcasted_iota(jnp.int32, (tm, tm), 1)
        before = jnp.where(r_i < c_i, 1.0, 0.0).astype(BF16)
        prefix = jnp.dot(sel.astype(BF16), before, preferred_element_type=F32) + carry_ref[...]
        for kk in range(TOP_K):
            idx_ref[0, kk:kk + 1, :] = idxs[kk]
            gate_ref[0, kk:kk + 1, :] = exps[kk] / denom
            rank = jnp.sum(jnp.where(e_iota == idxs[kk], prefix, 0.0), axis=0, keepdims=True)
            rank_ref[0, kk:kk + 1, :] = rank.astype(jnp.int32)
        total = carry_ref[...] + jnp.sum(sel, axis=1, keepdims=True)
        carry_ref[...] = total
        cnt_ref[...] = total.astype(jnp.int32)

    @pl.when(step == 0)
    def _():
        mix_norm(mix_out(mix_branches()))

    @pl.when((step > 0) & (step < n_tiles))
    def _():
        logits = route_logits()
        merged = mix_branches()
        picked = route_topk(logits)
        x1 = mix_out(merged)
        route_emit(*picked)
        mix_norm(x1)

    @pl.when(step == n_tiles)
    def _():
        route_emit(*route_topk(route_logits()))


def _merge(x, og, ol, gg, gl, mod, norm_w, wg, wl, wo, wrt, br, tm=256):
    bsz, s, d = x.shape
    ns = s // tm
    nt = bsz * ns
    cur = lambda t: jnp.minimum(t, nt - 1)
    tok = lambda n: pl.BlockSpec((1, tm, n), lambda t: (cur(t) // ns, cur(t) % ns, 0))
    full = lambda a: pl.BlockSpec(a.shape, lambda t: (0,) * a.ndim)
    lane = pl.BlockSpec((1, TOP_K, tm), lambda t: (jnp.maximum(t - 1, 0), 0, 0))
    return pl.pallas_call(
        _merge_kernel,
        grid=(nt + 1,),
        in_specs=[tok(d), tok(GDN_WIDTH), tok(LRU_WIDTH), tok(d), tok(d),
                  pl.BlockSpec((1, 6, d), lambda t: (cur(t) // ns, 0, 0)),
                  full(norm_w), full(wg), full(wl), full(wo), full(wrt), full(br)],
        out_specs=[tok(d),
                   pl.BlockSpec((tm, 1, d), lambda t: (cur(t), 0, 0)),
                   lane, lane, lane,
                   pl.BlockSpec((N_EXPERTS, 1), lambda t: (0, 0))],
        out_shape=[jax.ShapeDtypeStruct((bsz, s, d), F32),
                   jax.ShapeDtypeStruct((bsz * s, 1, d), F32),
                   jax.ShapeDtypeStruct((nt, TOP_K, tm), jnp.int32),
                   jax.ShapeDtypeStruct((nt, TOP_K, tm), F32),
                   jax.ShapeDtypeStruct((nt, TOP_K, tm), jnp.int32),
                   jax.ShapeDtypeStruct((N_EXPERTS, 1), jnp.int32)],
        scratch_shapes=[pltpu.VMEM((N_EXPERTS, 1), F32), pltpu.VMEM((tm, d), F32)],
        compiler_params=_cparams(("arbitrary",)),
        name="merge",
    )(x, og, ol, gg, gl, mod, norm_w, wg, wl, wo, wrt, br)


def _dispatch_kernel(dest_ref, zinfo_ref, h_ref, xs_ref, zbuf, sem, zsem):
    tm = h_ref.shape[0]
    n_blocks = xs_ref.shape[0] // MOE_BLOCK

    def row_copy(r, slot):
        return pltpu.make_async_copy(h_ref.at[r], xs_ref.at[slot], sem)

    def issue(r, carry):
        for kk in range(TOP_K):
            row_copy(r, dest_ref[0, 0, kk * tm + r]).start()
        return carry

    lax.fori_loop(0, tm, issue, 0)

    @pl.when(pl.program_id(0) == 0)
    def _():
        zbuf[...] = jnp.zeros(zbuf.shape, F32)

        def zero_copy(start, rows):
            return pltpu.make_async_copy(zbuf.at[pl.ds(0, rows)], xs_ref.at[pl.ds(start, rows)], zsem)

        def pad_runs(wait):
            def per_expert(e, carry):
                start = zinfo_ref[0, e]
                length = zinfo_ref[0, N_EXPERTS + e]
                rows = MOE_BLOCK // 2
                while rows >= 1:
                    @pl.when((length & rows) != 0)
                    def _(rows=rows):
                        cp = zero_copy(start + (length & (-2 * rows)), rows)
                        cp.wait() if wait else cp.start()
                    rows //= 2
                return carry

            lax.fori_loop(0, N_EXPERTS, per_expert, 0)

            def per_block(b, carry):
                cp = zero_copy(b * MOE_BLOCK, MOE_BLOCK)
                cp.wait() if wait else cp.start()
                return carry

            lax.fori_loop(zinfo_ref[0, 2 * N_EXPERTS], n_blocks, per_block, 0)

        pad_runs(False)
        pad_runs(True)

    def wait(r, carry):
        for kk in range(TOP_K):
            row_copy(r, 0).wait()
        return carry

    lax.fori_loop(0, tm, wait, 0)


def _dispatch(h2_rows, dest, zinfo, n_slots, tm=256):
    n_tok, _, d = h2_rows.shape
    nt = n_tok // tm
    return pl.pallas_call(
        _dispatch_kernel,
        grid=(nt,),
        in_specs=[pl.BlockSpec((1, 1, TOP_K * tm), lambda i: (i, 0, 0), memory_space=pltpu.SMEM),
                  pl.BlockSpec(zinfo.shape, lambda i: (0, 0), memory_space=pltpu.SMEM),
                  pl.BlockSpec((tm, 1, d), lambda i: (i, 0, 0))],
        out_specs=pl.BlockSpec(memory_space=pl.ANY),
        out_shape=jax.ShapeDtypeStruct((n_slots, 1, d), F32),
        scratch_shapes=[pltpu.VMEM((MOE_BLOCK, 1, d), F32), pltpu.SemaphoreType.DMA,
                        pltpu.SemaphoreType.DMA],
        compiler_params=_cparams(("arbitrary",)),
        name="dispatch",
    )(dest, zinfo, h2_rows)


def _expert_kernel(be_ref, nu_ref, win_ref, nval_ref, spare_ref, src_ref,
                   xs_ref, w1_ref, b1g_ref, b1l_ref, w2_ref, b2_ref, perm_ref,
                   yt_ref, x2d, w1p, w2b, yb0, yb1, yb2, sems):
    blk, _, d = xs_ref.shape
    f = w2b.shape[0]
    n_groups = 2 * f // EXPERT_COLS
    half = EXPERT_COLS // 2
    b = pl.program_id(0)
    n_blocks = pl.num_programs(0) - 1
    bufs = (yb0, yb1, yb2)
    live = b < nu_ref[0]
    b_w = jnp.minimum(b, n_blocks - 1)
    fresh = (b == 0) | (be_ref[b_w] != be_ref[jnp.maximum(b_w - 1, 0)])

    b_prev = jnp.maximum(b - 1, 0)
    win, n_val, spare = win_ref[b_prev], nval_ref[b_prev], spare_ref[b_prev]

    def row_copy(buf, sem, j):
        dst = jnp.where(j < n_val, src_ref[win + j], spare + j)
        return pltpu.make_async_copy(buf.at[j], yt_ref.at[dst], sem)

    def tile_wait(buf, sem):
        pltpu.make_async_copy(buf, yt_ref.at[pl.ds(0, blk)], sem).wait()

    def ffn(out_buf):
        x2d[...] = xs_ref[...].reshape(blk, d)
        u = jnp.dot(x2d[...].astype(BF16), w1p[...], preferred_element_type=F32)
        acts = []
        for j in range(n_groups):
            u_glu = u[:, j * EXPERT_COLS:j * EXPERT_COLS + half] + b1g_ref[0, :, j * half:(j + 1) * half]
            u_lin = u[:, j * EXPERT_COLS + half:(j + 1) * EXPERT_COLS] + b1l_ref[0, :, j * half:(j + 1) * half]
            u_glu = jnp.minimum(u_glu, SWIGLU_LIMIT)
            u_lin = jnp.clip(u_lin, -SWIGLU_LIMIT, SWIGLU_LIMIT)
            acts.append((u_glu * _sigmoid(SWIGLU_ALPHA * u_glu) * (u_lin + 1.0)).astype(BF16))
        act = jnp.concatenate(acts, axis=1)
        y = jnp.dot(act, w2b[...], preferred_element_type=F32) + b2_ref[0]
        out_buf[...] = y.reshape(blk, 1, d)

    @pl.when(live & fresh)
    def _():
        for j in range(n_groups):
            cols = slice(j * EXPERT_COLS, (j + 1) * EXPERT_COLS)
            w1p[:, cols] = jnp.dot(w1_ref[0, :, cols].astype(BF16), perm_ref[...],
                                   preferred_element_type=F32).astype(BF16)
        w2b[...] = w2_ref[0].astype(BF16)

    for p in range(3):
        cur, prv = bufs[p], bufs[(p + 2) % 3]
        sem_cur, sem_prv, sem_old = sems.at[p], sems.at[(p + 2) % 3], sems.at[(p + 1) % 3]
        mine = b % 3 == p

        @pl.when(mine & (b >= 3))
        def _(cur=cur, sem_cur=sem_cur):
            tile_wait(cur, sem_cur)

        @pl.when(mine & (b >= 1) & live)
        def _(cur=cur, prv=prv, sem_prv=sem_prv):
            for j in range(blk):
                row_copy(prv, sem_prv, j).start()
            ffn(cur)

        @pl.when(mine & ((b == 0) | jnp.logical_not(live)))
        def _(cur=cur, prv=prv, old=bufs[(p + 1) % 3], sem_prv=sem_prv, sem_old=sem_old):
            @pl.when(b >= 1)
            def _():
                def issue(j, carry):
                    row_copy(prv, sem_prv, j).start()
                    return carry

                lax.fori_loop(0, blk, issue, 0)

            @pl.when(live)
            def _():
                ffn(cur)

            @pl.when(jnp.logical_not(live) & (b < n_blocks))
            def _():
                cur[...] = jnp.zeros(cur.shape, F32)

            @pl.when(b == n_blocks)
            def _():
                tile_wait(old, sem_old)
                tile_wait(prv, sem_prv)


def _pair_split_matrix():
    i = jnp.arange(EXPERT_COLS)[:, None]
    j = jnp.arange(EXPERT_COLS)[None, :]
    half = EXPERT_COLS // 2
    return jnp.where(j < half, i == 2 * j, i == 2 * (j - half) + 1).astype(BF16)


def _experts(block_e, n_used, win_start, n_valid, spare0, src_sorted, xs, w1, b1g, b1l, w2, b2, n_rows_out):
    n_slots, _, d = xs.shape
    f = w2.shape[1]
    n_blocks = n_slots // MOE_BLOCK
    perm = _pair_split_matrix()
    slot = lambda b, be, nu, *_: (jnp.minimum(b, nu[0] - 1), 0, 0)
    wsel = lambda b, be, nu, *_: (be[jnp.minimum(b, n_blocks - 1)], 0, 0)
    row_buf = pltpu.VMEM((MOE_BLOCK, 1, d), F32)
    grid_spec = pltpu.PrefetchScalarGridSpec(
        num_scalar_prefetch=6,
        grid=(n_blocks + 1,),
        in_specs=[pl.BlockSpec((MOE_BLOCK, 1, d), slot),
                  pl.BlockSpec((1, d, 2 * f), wsel),
                  pl.BlockSpec((1, 1, f), wsel), pl.BlockSpec((1, 1, f), wsel),
                  pl.BlockSpec((1, f, d), wsel), pl.BlockSpec((1, 1, d), wsel),
                  pl.BlockSpec(perm.shape, lambda b, *_: (0, 0))],
        out_specs=pl.BlockSpec(memory_space=pl.ANY),
        scratch_shapes=[pltpu.VMEM((MOE_BLOCK, d), F32), pltpu.VMEM((d, 2 * f), BF16),
                        pltpu.VMEM((f, d), BF16), row_buf, row_buf, row_buf,
                        pltpu.SemaphoreType.DMA((3,))],
    )
    return pl.pallas_call(
        _expert_kernel,
        grid_spec=grid_spec,
        out_shape=jax.ShapeDtypeStruct((n_rows_out, 1, d), F32),
        compiler_params=pltpu.CompilerParams(dimension_semantics=("arbitrary",),
                                             vmem_limit_bytes=EXPERT_VMEM_LIMIT),
        name="experts",
    )(block_e, n_used, win_start, n_valid, spare0, src_sorted, xs, w1, b1g, b1l, w2, b2, perm)


def _combine_kernel(x1_ref, gate_ref, mod_ref, nw_ref, yt_ref, o_ref, buf2):
    tm = x1_ref.shape[1]
    d = x1_ref.shape[2]
    buf2[...] = yt_ref[...].reshape(TOP_K * tm, d)
    gate = gate_ref[0]
    acc = None
    for kk in range(TOP_K):
        term = gate[:, kk:kk + 1] * buf2[kk * tm:(kk + 1) * tm, :]
        acc = term if acc is None else acc + term
    x2 = x1_ref[0] + mod_ref[0, 5:6, :] * acc
    o_ref[0] = x2 * lax.rsqrt(jnp.mean(x2 * x2, axis=-1, keepdims=True) + EPS) * nw_ref[...]


def _combine(x1, gate, mod, norm_w, yt, tm=256):
    bsz, s, d = x1.shape
    ns = s // tm
    tok = lambda n: pl.BlockSpec((1, tm, n), lambda b, i: (b, i, 0))
    return pl.pallas_call(
        _combine_kernel,
        grid=(bsz, ns),
        in_specs=[tok(d),
                  pl.BlockSpec((1, tm, TOP_K), lambda b, i: (b * ns + i, 0, 0)),
                  pl.BlockSpec((1, 6, d), lambda b, i: (b, 0, 0)),
                  pl.BlockSpec((1, d), lambda b, i: (0, 0)),
                  pl.BlockSpec((TOP_K * tm, 1, d), lambda b, i: (b * ns + i, 0, 0))],
        out_specs=tok(d),
        out_shape=jax.ShapeDtypeStruct((bsz, s, d), F32),
        scratch_shapes=[pltpu.VMEM((TOP_K * tm, d), F32)],
        compiler_params=_cparams(("arbitrary", "arbitrary")),
        name="combine",
    )(x1, gate, mod, norm_w, yt)


def _block_diag(w):
    nb, bi, bo = w.shape
    eye = jnp.eye(nb, dtype=w.dtype)
    return (eye[:, None, :, None] * w[:, :, None, :]).reshape(nb * bi, nb * bo)


def _layer(x, c, w_ada, b_ada, norm_mix, norm_ffn, w_in, gdn_conv_w, gdn_a_log, gdn_dt_bias,
           gdn_norm_w, lru_conv_w, lru_conv_b, lru_w_a, lru_b_a, lru_w_x, lru_b_x, lru_lambda,
           w_branch_gdn, w_branch_lru, w_out, w_router, b_router, w1, b1, w2, b2, norm_final):
    bsz, s, d = x.shape
    n_tok = bsz * s
    tm = 256
    row = lambda v: v.reshape(1, -1).astype(F32)

    mod = _ada(c, w_ada, b_ada).reshape(bsz, 6, d)

    o_ba = 4 * GDN_WIDTH
    w_main = jnp.concatenate([w_in[:, :o_ba], w_in[:, o_ba + 2 * GDN_HEADS:]], axis=1).astype(BF16)
    w_ba_t = jnp.pad(w_in[:, o_ba:o_ba + 2 * GDN_HEADS].T, ((0, 128 - 2 * GDN_HEADS), (0, 0))).astype(BF16)
    qkv, z, xb, yb, gg, gl, ba, bat = _inproj(x, mod, row(norm_mix), w_main, w_ba_t)

    zeros4 = jnp.zeros((GDN_HEADS,), F32)
    hvals = jnp.stack([jnp.concatenate([zeros4, gdn_a_log.astype(F32)]),
                       jnp.concatenate([zeros4, gdn_dt_bias.astype(F32)])])
    o_gdn, o_lru = _mixers(
        (qkv, z, ba, bat, gdn_conv_w.astype(F32), hvals, hvals.T, row(gdn_norm_w)),
        (xb, yb, lru_conv_w.astype(F32), row(lru_conv_b),
         _block_diag(lru_w_a).astype(BF16), row(lru_b_a),
         _block_diag(lru_w_x).astype(BF16), row(lru_b_x), row(lru_lambda)))

    x1, h2_rows, idx_t, gate_t, rank_t, counts = _merge(
        x, o_gdn, o_lru, gg, gl, mod, row(norm_ffn),
        w_branch_gdn.astype(BF16), w_branch_lru.astype(BF16), w_out.astype(BF16),
        w_router.T.astype(F32), b_router.reshape(-1, 1).astype(F32), tm=tm)

    counts = counts.reshape(N_EXPERTS)
    padded = (counts + MOE_BLOCK - 1) // MOE_BLOCK * MOE_BLOCK
    pend = jnp.cumsum(padded)
    pstart = pend - padded
    assert (n_tok * TOP_K) % MOE_BLOCK == 0 and s % tm == 0
    n_blocks = (n_tok * TOP_K) // MOE_BLOCK + N_EXPERTS
    n_slots = n_blocks * MOE_BLOCK
    block_first = jnp.arange(n_blocks, dtype=jnp.int32) * MOE_BLOCK
    block_e = jnp.minimum(jnp.sum(pend[None, :] <= block_first[:, None], axis=1),
                          N_EXPERTS - 1).astype(jnp.int32)
    n_used = (pend[-1:] // MOE_BLOCK).astype(jnp.int32)
    onehot = idx_t[..., None] == jnp.arange(N_EXPERTS, dtype=jnp.int32)
    dest = (rank_t + jnp.sum(jnp.where(onehot, pstart.astype(jnp.int32), 0), axis=-1)).astype(jnp.int32)
    dest = dest.reshape(n_tok // tm, 1, TOP_K * tm)
    zinfo = jnp.concatenate([pstart + counts, padded - counts, n_used,
                             jnp.zeros((128 - 2 * N_EXPERTS - 1,), jnp.int32)]).astype(jnp.int32)

    n_assign = n_tok * TOP_K
    _, src_sorted = lax.sort_key_val(dest.reshape(-1), jnp.arange(n_assign, dtype=jnp.int32))
    src_sorted = jnp.concatenate([src_sorted, jnp.zeros((MOE_BLOCK,), jnp.int32)])
    cstart = (jnp.cumsum(counts) - counts).astype(jnp.int32)
    seg_rank = block_first - pstart[block_e].astype(jnp.int32)
    n_valid = jnp.clip(counts[block_e].astype(jnp.int32) - seg_rank, 0, MOE_BLOCK).astype(jnp.int32)
    win_start = jnp.clip(cstart[block_e] + seg_rank, 0, n_assign).astype(jnp.int32)
    spare0 = (n_assign + block_first - cstart[block_e] - counts[block_e].astype(jnp.int32)).astype(jnp.int32)

    xs = _dispatch(h2_rows, dest, zinfo.reshape(1, 128), n_slots, tm=tm)
    yt = _experts(block_e, n_used, win_start, n_valid, spare0, src_sorted, xs, w1.astype(F32),
                  b1[:, None, 0::2].astype(F32), b1[:, None, 1::2].astype(F32),
                  w2.astype(F32), b2[:, None, :].astype(F32), n_assign + N_EXPERTS * MOE_BLOCK)
    gate = jnp.swapaxes(gate_t, 1, 2)
    return _combine(x1, gate, mod, row(norm_final), yt, tm=tm)


def kernel(x, c, w_ada, b_ada, norm_mix, norm_ffn, w_in, gdn_conv_w, gdn_a_log, gdn_dt_bias,
           gdn_norm_w, lru_conv_w, lru_conv_b, lru_w_a, lru_b_a, lru_w_x, lru_b_x, lru_lambda,
           w_branch_gdn, w_branch_lru, w_out, w_router, b_router, w1, b1, w2, b2, norm_final):
    assert w_ada.shape[0] == 1, "single-layer block"
    return _layer(x, c, w_ada[0], b_ada[0], norm_mix[0], norm_ffn[0], w_in[0], gdn_conv_w[0],
                  gdn_a_log[0], gdn_dt_bias[0], gdn_norm_w[0], lru_conv_w[0], lru_conv_b[0],
                  lru_w_a[0], lru_b_a[0], lru_w_x[0], lru_b_x[0], lru_lambda[0],
                  w_branch_gdn[0], w_branch_lru[0], w_out[0], w_router[0], b_router[0],
                  w1[0], b1[0], w2[0], b2[0], norm_final)
```

```python
import functools

import jax
import jax.numpy as jnp
from jax import lax
from jax.experimental import pallas as pl
from jax.experimental.pallas import tpu as pltpu

F32 = jnp.float32
BF16 = jnp.bfloat16

EPS = 1e-6
GDN_HEADS = 4
GDN_HEAD_DIM = 128
GDN_WIDTH = GDN_HEADS * GDN_HEAD_DIM
GDN_CHUNK = 64
LRU_WIDTH = 512
LRU_C = 8.0
LRU_GROUP = 128
CONV_WIDTH = 4
N_EXPERTS = 32
TOP_K = 4
SWIGLU_ALPHA = 1.702
SWIGLU_LIMIT = 7.0
MOE_BLOCK = 512
EXPERT_COLS = 256

HALO = 8
VMEM_LIMIT = 56 * 1024 * 1024
EXPERT_VMEM_LIMIT = 60 * 1024 * 1024


def _cparams(sem):
    return pltpu.CompilerParams(dimension_semantics=sem, vmem_limit_bytes=VMEM_LIMIT)


def _sigmoid(x):
    return 1.0 / (1.0 + jnp.exp(-x))


def _silu(x):
    return x * _sigmoid(x)


def _softplus(x):
    return jnp.maximum(x, 0.0) + jnp.log1p(jnp.exp(-jnp.abs(x)))


def _gelu_tanh(x):
    return 0.5 * x * (1.0 + jnp.tanh(0.7978845608028654 * (x + 0.044715 * (x * x * x))))


def _bdot(a, b):
    return jnp.dot(a.astype(BF16), b.astype(BF16), preferred_element_type=F32)


def _bdot_nt(a, b):
    return lax.dot_general(a.astype(BF16), b.astype(BF16), (((1,), (1,)), ((), ())),
                           preferred_element_type=F32)


def _bdot_tn(a, b):
    return lax.dot_general(a.astype(BF16), b.astype(BF16), (((0,), (0,)), ((), ())),
                           preferred_element_type=F32)


def _fdot(a, b):
    return jnp.dot(a, b, preferred_element_type=F32, precision=lax.Precision.HIGHEST)


def _ada_kernel(ct_ref, w_ref, b_ref, o_ref):
    ct = _silu(ct_ref[...])
    w = w_ref[...]
    for b in range(ct.shape[1]):
        o_ref[b:b + 1, :] = jnp.sum(ct[:, b:b + 1] * w, axis=0, keepdims=True) + b_ref[...]


def _ada(c, w_ada, b_ada):
    bsz, d = c.shape
    n = w_ada.shape[1]
    tn = 512
    return pl.pallas_call(
        _ada_kernel,
        grid=(n // tn,),
        in_specs=[pl.BlockSpec((d, bsz), lambda j: (0, 0)),
                  pl.BlockSpec((d, tn), lambda j: (0, j)),
                  pl.BlockSpec((1, tn), lambda j: (0, j))],
        out_specs=pl.BlockSpec((bsz, tn), lambda j: (0, j)),
        out_shape=jax.ShapeDtypeStruct((bsz, n), F32),
        compiler_params=_cparams(("arbitrary",)),
        name="ada",
    )(c.T, w_ada, b_ada.reshape(1, n))


def _inproj_kernel(x_ref, mod_ref, nw_ref, wm_ref, wba_ref,
                   qkv_ref, z_ref, xb_ref, yb_ref, gg_ref, gl_ref, ba_ref, bat_ref):
    x = x_ref[0]
    shift = mod_ref[0, 0:1, :]
    scale = mod_ref[0, 1:2, :]
    y = x * lax.rsqrt(jnp.mean(x * x, axis=-1, keepdims=True) + EPS)
    h = ((y * nw_ref[...]) * (1.0 + scale) + shift).astype(BF16)
    col = 0
    for ref in (qkv_ref, z_ref, xb_ref, yb_ref, gg_ref, gl_ref):
        n = ref.shape[-1]
        ref[0] = jnp.dot(h, wm_ref[:, col:col + n], preferred_element_type=F32).astype(ref.dtype)
        col += n
    wba = wba_ref[...]
    nt = (((1,), (1,)), ((), ()))
    bat_ref[0] = lax.dot_general(wba[0:8], h, nt, preferred_element_type=F32)
    ba_ref[0] = lax.dot_general(h, wba, nt, preferred_element_type=F32)[:, 0:8]


def _inproj(x, mod, norm_w, w_main, w_ba_t, ts=512):
    bsz, s, d = x.shape
    widths = (3 * GDN_WIDTH, GDN_WIDTH, LRU_WIDTH, LRU_WIDTH, d, d)
    tok = lambda n: pl.BlockSpec((1, ts, n), lambda b, i: (b, i, 0))
    out_shapes = [jax.ShapeDtypeStruct((bsz, s, n), BF16) for n in widths]
    out_shapes += [jax.ShapeDtypeStruct((bsz, s, 8), F32), jax.ShapeDtypeStruct((bsz, 8, s), F32)]
    out_specs = [tok(n) for n in widths]
    out_specs += [tok(8), pl.BlockSpec((1, 8, ts), lambda b, i: (b, 0, i))]
    return pl.pallas_call(
        _inproj_kernel,
        grid=(bsz, s // ts),
        in_specs=[tok(d),
                  pl.BlockSpec((1, 6, d), lambda b, i: (b, 0, 0)),
                  pl.BlockSpec((1, d), lambda b, i: (0, 0)),
                  pl.BlockSpec(w_main.shape, lambda b, i: (0, 0)),
                  pl.BlockSpec(w_ba_t.shape, lambda b, i: (0, 0))],
        out_specs=out_specs,
        out_shape=out_shapes,
        compiler_params=_cparams(("arbitrary", "arbitrary")),
        name="inproj",
    )(x, mod, norm_w, w_main, w_ba_t)


def _causal_conv(x_tile, w_ref, xbuf, cols=slice(None)):
    ts = x_tile.shape[0]
    xbuf[HALO:HALO + ts, cols] = x_tile
    acc = None
    for j in range(CONV_WIDTH):
        off = HALO - (CONV_WIDTH - 1) + j
        term = w_ref[j:j + 1, cols] * xbuf[off:off + ts, cols]
        acc = term if acc is None else acc + term
    xbuf[0:HALO, cols] = xbuf[ts:ts + HALO, cols]
    return acc


def _gdn_body(qkv_ref, z_ref, ba_ref, bat_ref, cw_ref, hrow_ref, hcol_ref, nw_ref, lvl_ref,
              o_ref, xbuf, s_ref, fillers=()):
    ts = qkv_ref.shape[1]
    c = GDN_CHUNK
    dk = GDN_HEAD_DIM
    qkv = _silu(_causal_conv(qkv_ref[0].astype(F32), cw_ref, xbuf))

    ba = ba_ref[0]
    g_col = -jnp.exp(hrow_ref[0:1, :]) * _softplus(ba + hrow_ref[1:2, :])
    beta_col = _sigmoid(ba)
    r_i = lax.broadcasted_iota(jnp.int32, (ts, ts), 0)
    c_i = lax.broadcasted_iota(jnp.int32, (ts, ts), 1)
    same = (r_i // c) == (c_i // c)
    incl = same & (c_i <= r_i)
    strict = same & (c_i < r_i)
    gc_col = _fdot(jnp.where(incl, 1.0, 0.0).astype(F32), g_col)
    g_row = -jnp.exp(hcol_ref[:, 0:1]) * _softplus(bat_ref[0] + hcol_ref[:, 1:2])
    gc_row = _fdot(g_row, jnp.where(same & (r_i <= c_i), 1.0, 0.0).astype(F32))
    eye = jnp.where(r_i == c_i, 1.0, 0.0).astype(F32)
    scale = dk ** -0.5
    nw = nw_ref[...]
    n_lvl = lvl_ref.shape[0]

    heads = range(GDN_HEADS)
    ks, gccs, a_bf, t_mat, qks, rhs, q_decs = [], [], [], [], [], [], []
    for h in heads:
        q = qkv[:, h * dk:(h + 1) * dk]
        k = qkv[:, GDN_WIDTH + h * dk:GDN_WIDTH + (h + 1) * dk]
        v = qkv[:, 2 * GDN_WIDTH + h * dk:2 * GDN_WIDTH + (h + 1) * dk]
        q = q * (lax.rsqrt(jnp.sum(q * q, axis=-1, keepdims=True) + EPS) * scale)
        k = k * lax.rsqrt(jnp.sum(k * k, axis=-1, keepdims=True) + EPS)
        beta = beta_col[:, h:h + 1]
        gcc = gc_col[:, GDN_HEADS + h:GDN_HEADS + h + 1]
        gcr = gc_row[GDN_HEADS + h:GDN_HEADS + h + 1, :]
        decay = jnp.exp(jnp.where(incl, gcc - gcr, 0.0))
        kb = k * beta
        m1 = _bdot_nt(jnp.concatenate([kb, q], axis=0), k)
        a_mat = jnp.where(strict, m1[:ts] * decay, 0.0)
        e_gc = jnp.exp(gcc)
        ks.append(k)
        gccs.append(gcc)
        qks.append(jnp.where(incl, m1[ts:] * decay, 0.0))
        a_bf.append(a_mat.astype(BF16))
        t_mat.append(eye - a_mat * lvl_ref[0].astype(F32))
        rhs.append(jnp.concatenate([v * beta, kb * e_gc], axis=1))
        q_decs.append(q * e_gc)
    for lv in range(1, n_lvl):
        t_bf = [t_mat[h].astype(BF16) for h in heads]
        inner = [_bdot(a_bf[h] * lvl_ref[lv], t_bf[h]) for h in heads]
        t_mat = [t_mat[h] - _bdot(t_bf[h], inner[h]) for h in heads]
    uws = [_bdot(t_mat[h], rhs[h]) for h in heads]
    states = [s_ref[h] for h in heads]
    pre = {}
    for n in range(ts // c):
        rows = slice(n * c, (n + 1) * c)
        for h in heads:
            gc_last = gccs[h][(n + 1) * c - 1:(n + 1) * c, :]
            k_dec = ks[h][rows] * jnp.exp(gc_last - gccs[h][rows])
            uw = uws[h][rows]
            qk_uw = _bdot(qks[h][rows, n * c:(n + 1) * c], uw)
            kd_uw = _bdot_tn(k_dec, uw)
            lhs = jnp.concatenate([kd_uw[:, dk:], q_decs[h][rows] - qk_uw[:, dk:]], axis=0)
            pre[n, h] = (jnp.exp(gc_last), lhs, kd_uw[:, :dk], qk_uw[:, :dk])
    for n in range(ts // c):
        rows = slice(n * c, (n + 1) * c)
        if n < len(fillers):
            fillers[n]()
        for h in heads:
            g_last, lhs, kd_u, qk_u = pre[n, h]
            prod = _bdot(lhs, states[h])
            o = prod[dk:] + qk_u
            states[h] = states[h] * g_last - prod[:dk] + kd_u
            zg = _silu(z_ref[0, rows, h * dk:(h + 1) * dk].astype(F32))
            o = o * lax.rsqrt(jnp.mean(o * o, axis=-1, keepdims=True) + EPS) * nw * zg
            o_ref[0, rows, h * dk:(h + 1) * dk] = o.astype(o_ref.dtype)
    for h in heads:
        s_ref[h] = states[h]


def _gdn_level_masks(ts):
    i = jnp.arange(ts)[:, None]
    j = jnp.arange(ts)[None, :]
    masks = []
    s = 1
    while s < GDN_CHUNK:
        masks.append(((i // (2 * s)) == (j // (2 * s))) & ((i // s) != (j // s)))
        s *= 2
    return jnp.stack(masks).astype(BF16)


def _lru_body(xb_ref, yb_ref, cw_ref, cb_ref, wa_ref, ba_ref, wx_ref, bx_ref, lam_ref,
              o_ref, xbuf, carry_ref, cols):
    ts = xb_ref.shape[1]
    xc = _causal_conv(xb_ref[0, :, cols].astype(F32), cw_ref, xbuf, cols) + cb_ref[:, cols]
    r = _sigmoid(_bdot(xc, wa_ref[cols, cols]) + ba_ref[:, cols])
    i = _sigmoid(_bdot(xc, wx_ref[cols, cols]) + bx_ref[:, cols])
    log_a = (-LRU_C) * r * _softplus(-lam_ref[:, cols])
    a = jnp.exp(log_a)
    mult = jnp.sqrt(jnp.maximum(1.0 - jnp.exp(2.0 * log_a), 0.0))
    b = mult * (i * xc)
    row = lax.broadcasted_iota(jnp.int32, a.shape, 0)
    d = 1
    while d < ts:
        keep = row >= d
        a_s = jnp.where(keep, pltpu.roll(a, d, axis=0), 1.0)
        b_s = jnp.where(keep, pltpu.roll(b, d, axis=0), 0.0)
        b = a * b_s + b
        a = a * a_s
        d *= 2
    h = a * carry_ref[:, cols] + b
    carry_ref[:, cols] = h[ts - 1:ts, :]
    o_ref[0, :, cols] = (h * _gelu_tanh(yb_ref[0, :, cols].astype(F32))).astype(o_ref.dtype)


N_GDN_IN, N_LRU_IN = 9, 9


def _mixers_kernel(*refs):
    gdn_in = refs[:N_GDN_IN]
    lru_in = refs[N_GDN_IN:N_GDN_IN + N_LRU_IN]
    o_gdn_ref, o_lru_ref, gdn_xbuf, s_ref, lru_xbuf, carry_ref = refs[N_GDN_IN + N_LRU_IN:]

    @pl.when(pl.program_id(1) == 0)
    def _():
        s_ref[...] = jnp.zeros(s_ref.shape, F32)
        carry_ref[...] = jnp.zeros(carry_ref.shape, F32)
        gdn_xbuf[0:HALO, :] = jnp.zeros((HALO, gdn_xbuf.shape[1]), F32)
        lru_xbuf[0:HALO, :] = jnp.zeros((HALO, lru_xbuf.shape[1]), F32)

    groups = [slice(g * LRU_GROUP, (g + 1) * LRU_GROUP) for g in range(LRU_WIDTH // LRU_GROUP)]
    _gdn_body(*gdn_in, o_gdn_ref, gdn_xbuf, s_ref,
              fillers=[functools.partial(_lru_body, *lru_in, o_lru_ref, lru_xbuf, carry_ref, cols)
                       for cols in groups])


def _mixers(gdn_args, lru_args, ts=256):
    qkv, z, ba, bat = gdn_args[:4]
    xb, yb = lru_args[:2]
    bsz, s, _ = qkv.shape
    lvl = _gdn_level_masks(ts)
    gdn_args = tuple(gdn_args) + (lvl,)
    assert len(gdn_args) == N_GDN_IN and len(lru_args) == N_LRU_IN
    tok = lambda n: pl.BlockSpec((1, ts, n), lambda b, i: (b, i, 0))
    full = lambda a: pl.BlockSpec(a.shape, lambda b, i: (0,) * a.ndim)
    in_specs = ([tok(3 * GDN_WIDTH), tok(GDN_WIDTH), tok(8), pl.BlockSpec((1, 8, ts), lambda b, i: (b, 0, i))]
                + [full(a) for a in gdn_args[4:]]
                + [tok(LRU_WIDTH), tok(LRU_WIDTH)] + [full(a) for a in lru_args[2:]])
    return pl.pallas_call(
        _mixers_kernel,
        grid=(bsz, s // ts),
        in_specs=in_specs,
        out_specs=[tok(GDN_WIDTH), tok(LRU_WIDTH)],
        out_shape=[jax.ShapeDtypeStruct((bsz, s, GDN_WIDTH), BF16),
                   jax.ShapeDtypeStruct((bsz, s, LRU_WIDTH), BF16)],
        scratch_shapes=[pltpu.VMEM((ts + HALO, 3 * GDN_WIDTH), F32),
                        pltpu.VMEM((GDN_HEADS, GDN_HEAD_DIM, GDN_HEAD_DIM), F32),
                        pltpu.VMEM((ts + HALO, LRU_WIDTH), F32), pltpu.VMEM((1, LRU_WIDTH), F32)],
        compiler_params=_cparams(("arbitrary", "arbitrary")),
        name="mixers",
    )(*gdn_args, *lru_args)


def _merge_kernel(x_ref, og_ref, ol_ref, gg_ref, gl_ref, mod_ref, nw_ref,
                  wg_ref, wl_ref, wo_ref, wrt_ref, br_ref,
                  x1_ref, h2_ref, idx_ref, gate_ref, rank_ref, cnt_ref, carry_ref, hprev_ref):
    tm = x_ref.shape[1]
    step = pl.program_id(0)
    n_tiles = pl.num_programs(0) - 1

    @pl.when(step == 0)
    def _():
        carry_ref[...] = jnp.zeros(carry_ref.shape, F32)

    def mix_branches():
        return (_sigmoid(gg_ref[0].astype(F32)) * jnp.dot(og_ref[0], wg_ref[...], preferred_element_type=F32)
                + _sigmoid(gl_ref[0].astype(F32)) * jnp.dot(ol_ref[0], wl_ref[...], preferred_element_type=F32))

    def mix_out(merged):
        x1 = x_ref[0] + mod_ref[0, 2:3, :] * _bdot(merged, wo_ref[...])
        x1_ref[0] = x1
        return x1

    def mix_norm(x1):
        y = x1 * lax.rsqrt(jnp.mean(x1 * x1, axis=-1, keepdims=True) + EPS)
        h2 = (y * nw_ref[...]) * (1.0 + mod_ref[0, 4:5, :]) + mod_ref[0, 3:4, :]
        h2_ref[...] = h2.reshape(tm, 1, h2.shape[-1])
        hprev_ref[...] = h2

    def route_logits():
        return lax.dot_general(wrt_ref[...], hprev_ref[...], (((1,), (1,)), ((), ())),
                               preferred_element_type=F32, precision=lax.Precision.HIGHEST) + br_ref[...]

    def route_topk(logits):
        e_iota = lax.broadcasted_iota(jnp.int32, logits.shape, 0)
        work = logits
        sel = jnp.zeros(logits.shape, F32)
        vals, idxs = [], []
        for _ in range(TOP_K):
            m = jnp.max(work, axis=0, keepdims=True)
            idx = jnp.min(jnp.where(work == m, e_iota, N_EXPERTS), axis=0, keepdims=True)
            hit = e_iota == idx
            sel = jnp.where(hit, 1.0, sel)
            work = jnp.where(hit, -jnp.inf, work)
            vals.append(m)
            idxs.append(idx)
        return e_iota, sel, vals, idxs

    def route_emit(e_iota, sel, vals, idxs):
        exps = [jnp.exp(v - vals[0]) for v in vals]
        denom = exps[0] + exps[1] + exps[2] + exps[3]
        r_i = lax.broadcasted_iota(jnp.int32, (tm, tm), 0)
        c_i = lax.broadcasted_iota(jnp.int32, (tm, tm), 1)
        before = jnp.where(r_i < c_i, 1.0, 0.0).astype(BF16)
        prefix = jnp.dot(sel.astype(BF16), before, preferred_element_type=F32) + carry_ref[...]
        for kk in range(TOP_K):
            idx_ref[0, kk:kk + 1, :] = idxs[kk]
            gate_ref[0, kk:kk + 1, :] = exps[kk] / denom
            rank = jnp.sum(jnp.where(e_iota == idxs[kk], prefix, 0.0), axis=0, keepdims=True)
            rank_ref[0, kk:kk + 1, :] = rank.astype(jnp.int32)
        total = carry_ref[...] + jnp.sum(sel, axis=1, keepdims=True)
        carry_ref[...] = total
        cnt_ref[...] = total.astype(jnp.int32)

    @pl.when(step == 0)
    def _():
        mix_norm(mix_out(mix_branches()))

    @pl.when((step > 0) & (step < n_tiles))
    def _():
        logits = route_logits()
        merged = mix_branches()
        picked = route_topk(logits)
        x1 = mix_out(merged)
        route_emit(*picked)
        mix_norm(x1)

    @pl.when(step == n_tiles)
    def _():
        route_emit(*route_topk(route_logits()))


def _merge(x, og, ol, gg, gl, mod, norm_w, wg, wl, wo, wrt, br, tm=256):
    bsz, s, d = x.shape
    ns = s // tm
    nt = bsz * ns
    cur = lambda t: jnp.minimum(t, nt - 1)
    tok = lambda n: pl.BlockSpec((1, tm, n), lambda t: (cur(t) // ns, cur(t) % ns, 0))
    full = lambda a: pl.BlockSpec(a.shape, lambda t: (0,) * a.ndim)
    lane = pl.BlockSpec((1, TOP_K, tm), lambda t: (jnp.maximum(t - 1, 0), 0, 0))
    return pl.pallas_call(
        _merge_kernel,
        grid=(nt + 1,),
        in_specs=[tok(d), tok(GDN_WIDTH), tok(LRU_WIDTH), tok(d), tok(d),
                  pl.BlockSpec((1, 6, d), lambda t: (cur(t) // ns, 0, 0)),
                  full(norm_w), full(wg), full(wl), full(wo), full(wrt), full(br)],
        out_specs=[tok(d),
                   pl.BlockSpec((tm, 1, d), lambda t: (cur(t), 0, 0)),
                   lane, lane, lane,
                   pl.BlockSpec((N_EXPERTS, 1), lambda t: (0, 0))],
        out_shape=[jax.ShapeDtypeStruct((bsz, s, d), F32),
                   jax.ShapeDtypeStruct((bsz * s, 1, d), F32),
                   jax.ShapeDtypeStruct((nt, TOP_K, tm), jnp.int32),
                   jax.ShapeDtypeStruct((nt, TOP_K, tm), F32),
                   jax.ShapeDtypeStruct((nt, TOP_K, tm), jnp.int32),
                   jax.ShapeDtypeStruct((N_EXPERTS, 1), jnp.int32)],
        scratch_shapes=[pltpu.VMEM((N_EXPERTS, 1), F32), pltpu.VMEM((tm, d), F32)],
        compiler_params=_cparams(("arbitrary",)),
        name="merge",
    )(x, og, ol, gg, gl, mod, norm_w, wg, wl, wo, wrt, br)


def _dispatch_kernel(dest_ref, zinfo_ref, h_ref, xs_ref, zbuf, sem, zsem):
    tm = h_ref.shape[0]
    n_blocks = xs_ref.shape[0] // MOE_BLOCK

    def row_copy(r, slot):
        return pltpu.make_async_copy(h_ref.at[r], xs_ref.at[slot], sem)

    def issue(r, carry):
        for kk in range(TOP_K):
            row_copy(r, dest_ref[0, 0, kk * tm + r]).start()
        return carry

    lax.fori_loop(0, tm, issue, 0)

    @pl.when(pl.program_id(0) == 0)
    def _():
        zbuf[...] = jnp.zeros(zbuf.shape, F32)

        def zero_copy(start, rows):
            return pltpu.make_async_copy(zbuf.at[pl.ds(0, rows)], xs_ref.at[pl.ds(start, rows)], zsem)

        def pad_runs(wait):
            def per_expert(e, carry):
                start = zinfo_ref[0, e]
                length = zinfo_ref[0, N_EXPERTS + e]
                rows = MOE_BLOCK // 2
                while rows >= 1:
                    @pl.when((length & rows) != 0)
                    def _(rows=rows):
                        cp = zero_copy(start + (length & (-2 * rows)), rows)
                        cp.wait() if wait else cp.start()
                    rows //= 2
                return carry

            lax.fori_loop(0, N_EXPERTS, per_expert, 0)

            def per_block(b, carry):
                cp = zero_copy(b * MOE_BLOCK, MOE_BLOCK)
                cp.wait() if wait else cp.start()
                return carry

            lax.fori_loop(zinfo_ref[0, 2 * N_EXPERTS], n_blocks, per_block, 0)

        pad_runs(False)
        pad_runs(True)

    def wait(r, carry):
        for kk in range(TOP_K):
            row_copy(r, 0).wait()
        return carry

    lax.fori_loop(0, tm, wait, 0)


def _dispatch(h2_rows, dest, zinfo, n_slots, tm=256):
    n_tok, _, d = h2_rows.shape
    nt = n_tok // tm
    return pl.pallas_call(
        _dispatch_kernel,
        grid=(nt,),
        in_specs=[pl.BlockSpec((1, 1, TOP_K * tm), lambda i: (i, 0, 0), memory_space=pltpu.SMEM),
                  pl.BlockSpec(zinfo.shape, lambda i: (0, 0), memory_space=pltpu.SMEM),
                  pl.BlockSpec((tm, 1, d), lambda i: (i, 0, 0))],
        out_specs=pl.BlockSpec(memory_space=pl.ANY),
        out_shape=jax.ShapeDtypeStruct((n_slots, 1, d), F32),
        scratch_shapes=[pltpu.VMEM((MOE_BLOCK, 1, d), F32), pltpu.SemaphoreType.DMA,
                        pltpu.SemaphoreType.DMA],
        compiler_params=_cparams(("arbitrary",)),
        name="dispatch",
    )(dest, zinfo, h2_rows)


def _expert_kernel(be_ref, nu_ref, win_ref, nval_ref, spare_ref, src_ref,
                   xs_ref, w1_ref, b1g_ref, b1l_ref, w2_ref, b2_ref, perm_ref,
                   yt_ref, x2d, w1p, w2b, yb0, yb1, yb2, sems):
    blk, _, d = xs_ref.shape
    f = w2b.shape[0]
    n_groups = 2 * f // EXPERT_COLS
    half = EXPERT_COLS // 2
    b = pl.program_id(0)
    n_blocks = pl.num_programs(0) - 1
    bufs = (yb0, yb1, yb2)
    live = b < nu_ref[0]
    b_w = jnp.minimum(b, n_blocks - 1)
    fresh = (b == 0) | (be_ref[b_w] != be_ref[jnp.maximum(b_w - 1, 0)])

    b_prev = jnp.maximum(b - 1, 0)
    win, n_val, spare = win_ref[b_prev], nval_ref[b_prev], spare_ref[b_prev]

    def row_copy(buf, sem, j):
        dst = jnp.where(j < n_val, src_ref[win + j], spare + j)
        return pltpu.make_async_copy(buf.at[j], yt_ref.at[dst], sem)

    def tile_wait(buf, sem):
        pltpu.make_async_copy(buf, yt_ref.at[pl.ds(0, blk)], sem).wait()

    def ffn(out_buf):
        x2d[...] = xs_ref[...].reshape(blk, d)
        u = jnp.dot(x2d[...].astype(BF16), w1p[...], preferred_element_type=F32)
        acts = []
        for j in range(n_groups):
            u_glu = u[:, j * EXPERT_COLS:j * EXPERT_COLS + half] + b1g_ref[0, :, j * half:(j + 1) * half]
            u_lin = u[:, j * EXPERT_COLS + half:(j + 1) * EXPERT_COLS] + b1l_ref[0, :, j * half:(j + 1) * half]
            u_glu = jnp.minimum(u_glu, SWIGLU_LIMIT)
            u_lin = jnp.clip(u_lin, -SWIGLU_LIMIT, SWIGLU_LIMIT)
            acts.append((u_glu * _sigmoid(SWIGLU_ALPHA * u_glu) * (u_lin + 1.0)).astype(BF16))
        act = jnp.concatenate(acts, axis=1)
        y = jnp.dot(act, w2b[...], preferred_element_type=F32) + b2_ref[0]
        out_buf[...] = y.reshape(blk, 1, d)

    @pl.when(live & fresh)
    def _():
        for j in range(n_groups):
            cols = slice(j * EXPERT_COLS, (j + 1) * EXPERT_COLS)
            w1p[:, cols] = jnp.dot(w1_ref[0, :, cols].astype(BF16), perm_ref[...],
                                   preferred_element_type=F32).astype(BF16)
        w2b[...] = w2_ref[0].astype(BF16)

    for p in range(3):
        cur, prv = bufs[p], bufs[(p + 2) % 3]
        sem_cur, sem_prv, sem_old = sems.at[p], sems.at[(p + 2) % 3], sems.at[(p + 1) % 3]
        mine = b % 3 == p

        @pl.when(mine & (b >= 3))
        def _(cur=cur, sem_cur=sem_cur):
            tile_wait(cur, sem_cur)

        @pl.when(mine & (b >= 1) & live)
        def _(cur=cur, prv=prv, sem_prv=sem_prv):
            for j in range(blk):
                row_copy(prv, sem_prv, j).start()
            ffn(cur)

        @pl.when(mine & ((b == 0) | jnp.logical_not(live)))
        def _(cur=cur, prv=prv, old=bufs[(p + 1) % 3], sem_prv=sem_prv, sem_old=sem_old):
            @pl.when(b >= 1)
            def _():
                def issue(j, carry):
                    row_copy(prv, sem_prv, j).start()
                    return carry

                lax.fori_loop(0, blk, issue, 0)

            @pl.when(live)
            def _():
                ffn(cur)

            @pl.when(jnp.logical_not(live) & (b < n_blocks))
            def _():
                cur[...] = jnp.zeros(cur.shape, F32)

            @pl.when(b == n_blocks)
            def _():
                tile_wait(old, sem_old)
                tile_wait(prv, sem_prv)


def _pair_split_matrix():
    i = jnp.arange(EXPERT_COLS)[:, None]
    j = jnp.arange(EXPERT_COLS)[None, :]
    half = EXPERT_COLS // 2
    return jnp.where(j < half, i == 2 * j, i == 2 * (j - half) + 1).astype(BF16)


def _experts(block_e, n_used, win_start, n_valid, spare0, src_sorted, xs, w1, b1g, b1l, w2, b2, n_rows_out):
    n_slots, _, d = xs.shape
    f = w2.shape[1]
    n_blocks = n_slots // MOE_BLOCK
    perm = _pair_split_matrix()
    slot = lambda b, be, nu, *_: (jnp.minimum(b, nu[0] - 1), 0, 0)
    wsel = lambda b, be, nu, *_: (be[jnp.minimum(b, n_blocks - 1)], 0, 0)
    row_buf = pltpu.VMEM((MOE_BLOCK, 1, d), F32)
    grid_spec = pltpu.PrefetchScalarGridSpec(
        num_scalar_prefetch=6,
        grid=(n_blocks + 1,),
        in_specs=[pl.BlockSpec((MOE_BLOCK, 1, d), slot),
                  pl.BlockSpec((1, d, 2 * f), wsel),
                  pl.BlockSpec((1, 1, f), wsel), pl.BlockSpec((1, 1, f), wsel),
                  pl.BlockSpec((1, f, d), wsel), pl.BlockSpec((1, 1, d), wsel),
                  pl.BlockSpec(perm.shape, lambda b, *_: (0, 0))],
        out_specs=pl.BlockSpec(memory_space=pl.ANY),
        scratch_shapes=[pltpu.VMEM((MOE_BLOCK, d), F32), pltpu.VMEM((d, 2 * f), BF16),
                        pltpu.VMEM((f, d), BF16), row_buf, row_buf, row_buf,
                        pltpu.SemaphoreType.DMA((3,))],
    )
    return pl.pallas_call(
        _expert_kernel,
        grid_spec=grid_spec,
        out_shape=jax.ShapeDtypeStruct((n_rows_out, 1, d), F32),
        compiler_params=pltpu.CompilerParams(dimension_semantics=("arbitrary",),
                                             vmem_limit_bytes=EXPERT_VMEM_LIMIT),
        name="experts",
    )(block_e, n_used, win_start, n_valid, spare0, src_sorted, xs, w1, b1g, b1l, w2, b2, perm)


def _combine_kernel(x1_ref, gate_ref, mod_ref, nw_ref, yt_ref, o_ref, buf2):
    tm = x1_ref.shape[1]
    d = x1_ref.shape[2]
    buf2[...] = yt_ref[...].reshape(TOP_K * tm, d)
    gate = gate_ref[0]
    acc = None
    for kk in range(TOP_K):
        term = gate[:, kk:kk + 1] * buf2[kk * tm:(kk + 1) * tm, :]
        acc = term if acc is None else acc + term
    x2 = x1_ref[0] + mod_ref[0, 5:6, :] * acc
    o_ref[0] = x2 * lax.rsqrt(jnp.mean(x2 * x2, axis=-1, keepdims=True) + EPS) * nw_ref[...]


def _combine(x1, gate, mod, norm_w, yt, tm=256):
    bsz, s, d = x1.shape
    ns = s // tm
    tok = lambda n: pl.BlockSpec((1, tm, n), lambda b, i: (b, i, 0))
    return pl.pallas_call(
        _combine_kernel,
        grid=(bsz, ns),
        in_specs=[tok(d),
                  pl.BlockSpec((1, tm, TOP_K), lambda b, i: (b * ns + i, 0, 0)),
                  pl.BlockSpec((1, 6, d), lambda b, i: (b, 0, 0)),
                  pl.BlockSpec((1, d), lambda b, i: (0, 0)),
                  pl.BlockSpec((TOP_K * tm, 1, d), lambda b, i: (b * ns + i, 0, 0))],
        out_specs=tok(d),
        out_shape=jax.ShapeDtypeStruct((bsz, s, d), F32),
        scratch_shapes=[pltpu.VMEM((TOP_K * tm, d), F32)],
        compiler_params=_cparams(("arbitrary", "arbitrary")),
        name="combine",
    )(x1, gate, mod, norm_w, yt)


def _block_diag(w):
    nb, bi, bo = w.shape
    eye = jnp.eye(nb, dtype=w.dtype)
    return (eye[:, None, :, None] * w[:, :, None, :]).reshape(nb * bi, nb * bo)


def _layer(x, c, w_ada, b_ada, norm_mix, norm_ffn, w_in, gdn_conv_w, gdn_a_log, gdn_dt_bias,
           gdn_norm_w, lru_conv_w, lru_conv_b, lru_w_a, lru_b_a, lru_w_x, lru_b_x, lru_lambda,
           w_branch_gdn, w_branch_lru, w_out, w_router, b_router, w1, b1, w2, b2, norm_final):
    bsz, s, d = x.shape
    n_tok = bsz * s
    tm = 256
    row = lambda v: v.reshape(1, -1).astype(F32)

    mod = _ada(c, w_ada, b_ada).reshape(bsz, 6, d)

    o_ba = 4 * GDN_WIDTH
    w_main = jnp.concatenate([w_in[:, :o_ba], w_in[:, o_ba + 2 * GDN_HEADS:]], axis=1).astype(BF16)
    w_ba_t = jnp.pad(w_in[:, o_ba:o_ba + 2 * GDN_HEADS].T, ((0, 128 - 2 * GDN_HEADS), (0, 0))).astype(BF16)
    qkv, z, xb, yb, gg, gl, ba, bat = _inproj(x, mod, row(norm_mix), w_main, w_ba_t)

    zeros4 = jnp.zeros((GDN_HEADS,), F32)
    hvals = jnp.stack([jnp.concatenate([zeros4, gdn_a_log.astype(F32)]),
                       jnp.concatenate([zeros4, gdn_dt_bias.astype(F32)])])
    o_gdn, o_lru = _mixers(
        (qkv, z, ba, bat, gdn_conv_w.astype(F32), hvals, hvals.T, row(gdn_norm_w)),
        (xb, yb, lru_conv_w.astype(F32), row(lru_conv_b),
         _block_diag(lru_w_a).astype(BF16), row(lru_b_a),
         _block_diag(lru_w_x).astype(BF16), row(lru_b_x), row(lru_lambda)))

    x1, h2_rows, idx_t, gate_t, rank_t, counts = _merge(
        x, o_gdn, o_lru, gg, gl, mod, row(norm_ffn),
        w_branch_gdn.astype(BF16), w_branch_lru.astype(BF16), w_out.astype(BF16),
        w_router.T.astype(F32), b_router.reshape(-1, 1).astype(F32), tm=tm)

    counts = counts.reshape(N_EXPERTS)
    padded = (counts + MOE_BLOCK - 1) // MOE_BLOCK * MOE_BLOCK
    pend = jnp.cumsum(padded)
    pstart = pend - padded
    assert (n_tok * TOP_K) % MOE_BLOCK == 0 and s % tm == 0
    n_blocks = (n_tok * TOP_K) // MOE_BLOCK + N_EXPERTS
    n_slots = n_blocks * MOE_BLOCK
    block_first = jnp.arange(n_blocks, dtype=jnp.int32) * MOE_BLOCK
    block_e = jnp.minimum(jnp.sum(pend[None, :] <= block_first[:, None], axis=1),
                          N_EXPERTS - 1).astype(jnp.int32)
    n_used = (pend[-1:] // MOE_BLOCK).astype(jnp.int32)
    onehot = idx_t[..., None] == jnp.arange(N_EXPERTS, dtype=jnp.int32)
    dest = (rank_t + jnp.sum(jnp.where(onehot, pstart.astype(jnp.int32), 0), axis=-1)).astype(jnp.int32)
    dest = dest.reshape(n_tok // tm, 1, TOP_K * tm)
    zinfo = jnp.concatenate([pstart + counts, padded - counts, n_used,
                             jnp.zeros((128 - 2 * N_EXPERTS - 1,), jnp.int32)]).astype(jnp.int32)

    n_assign = n_tok * TOP_K
    nt = n_tok // tm
    tile_i = jnp.arange(nt, dtype=jnp.int32)[:, None, None]
    k_i = jnp.arange(TOP_K, dtype=jnp.int32)[None, :, None]
    r_i = jnp.arange(tm, dtype=jnp.int32)[None, None, :]
    keys = idx_t * n_assign + (tile_i * tm + r_i) * TOP_K + k_i
    low = jnp.sort(keys.reshape(-1)) % n_assign
    tok_s, k_s = low // TOP_K, low % TOP_K
    src_sorted = ((tok_s // tm) * (TOP_K * tm) + k_s * tm + tok_s % tm).astype(jnp.int32)
    src_sorted = jnp.concatenate([src_sorted, jnp.zeros((MOE_BLOCK,), jnp.int32)])
    cstart = (jnp.cumsum(counts) - counts).astype(jnp.int32)
    seg_rank = block_first - pstart[block_e].astype(jnp.int32)
    n_valid = jnp.clip(counts[block_e].astype(jnp.int32) - seg_rank, 0, MOE_BLOCK).astype(jnp.int32)
    win_start = jnp.clip(cstart[block_e] + seg_rank, 0, n_assign).astype(jnp.int32)
    spare0 = (n_assign + block_first - cstart[block_e] - counts[block_e].astype(jnp.int32)).astype(jnp.int32)

    xs = _dispatch(h2_rows, dest, zinfo.reshape(1, 128), n_slots, tm=tm)
    yt = _experts(block_e, n_used, win_start, n_valid, spare0, src_sorted, xs, w1.astype(F32),
                  b1[:, None, 0::2].astype(F32), b1[:, None, 1::2].astype(F32),
                  w2.astype(F32), b2[:, None, :].astype(F32), n_assign + N_EXPERTS * MOE_BLOCK)
    gate = jnp.swapaxes(gate_t, 1, 2)
    return _combine(x1, gate, mod, row(norm_final), yt, tm=tm)


def kernel(x, c, w_ada, b_ada, norm_mix, norm_ffn, w_in, gdn_conv_w, gdn_a_log, gdn_dt_bias,
           gdn_norm_w, lru_conv_w, lru_conv_b, lru_w_a, lru_b_a, lru_w_x, lru_b_x, lru_lambda,
           w_branch_gdn, w_branch_lru, w_out, w_router, b_router, w1, b1, w2, b2, norm_final):
    assert w_ada.shape[0] == 1, "single-layer block"
    return _layer(x, c, w_ada[0], b_ada[0], norm_mix[0], norm_ffn[0], w_in[0], gdn_conv_w[0],
                  gdn_a_log[0], gdn_dt_bias[0], gdn_norm_w[0], lru_conv_w[0], lru_conv_b[0],
                  lru_w_a[0], lru_b_a[0], lru_w_x[0], lru_b_x[0], lru_lambda[0],
                  w_branch_gdn[0], w_branch_lru[0], w_out[0], w_router[0], b_router[0],
                  w1[0], b1[0], w2[0], b2[0], norm_final)
```
